```python
import math
import jax, jax.numpy as jnp
from jax import lax
import numpy as np

D_MODEL = 4096
BATCH = 4
SEQ = 2048
DEPTH = 4
DEC_BATCH = 128
DEC_SEQ = 1
PAST_LEN = 16384
PAGE_SIZE = 128

MIX_WIDTH = 2 * D_MODEL
SSD_WIDTH = MIX_WIDTH // 2
RET_WIDTH = MIX_WIDTH - SSD_WIDTH
SSD_HEADDIM = 64
SSD_HEADS = SSD_WIDTH // SSD_HEADDIM
SSD_GROUPS = 8
SSD_HEADS_PER_GROUP = SSD_HEADS // SSD_GROUPS
SSD_STATE = 128
SSD_NORM_GROUPS = SSD_GROUPS
CONV_K = 4
CONV_DIM = SSD_WIDTH + 2 * SSD_GROUPS * SSD_STATE
RET_HEADS = 16
RET_DV = RET_WIDTH // RET_HEADS
RET_DK = RET_DV // 2
RET_QK_WIDTH = RET_HEADS * RET_DK
IN_COLS = SSD_WIDTH + CONV_DIM + SSD_HEADS + 2 * RET_QK_WIDTH + 2 * RET_WIDTH
CHUNK = 128
ROPE_BASE = 10000.0
EPS = 1e-5

kernel_name = "hymba_ssd_retention_decode_step"


def _rms_norm(x, w):
    xf = x.astype(jnp.float32)
    y = xf * lax.rsqrt(jnp.mean(xf * xf, axis=-1, keepdims=True) + EPS)
    return (y * w.astype(jnp.float32)).astype(x.dtype)


def _retention_log_decay():
    return jnp.log1p(-jnp.exp2(-5.0 - jnp.arange(RET_HEADS, dtype=jnp.float32)))


def _rotary(x, positions):
    half = x.shape[-1] // 2
    inv = 1.0 / (ROPE_BASE ** (jnp.arange(half, dtype=jnp.float32) / half))
    ang = positions.astype(jnp.float32)[:, None] * inv[None, :]
    cos = jnp.cos(ang)[None, :, None, :]
    sin = jnp.sin(ang)[None, :, None, :]
    xf = x.astype(jnp.float32)
    x1, x2 = xf[..., :half], xf[..., half:]
    return jnp.concatenate([x1 * cos - x2 * sin, x2 * cos + x1 * sin], axis=-1)


def _chunked_decay_scan(q, k, v, log_a, s0):
    b, T, G, N = q.shape
    R, P = v.shape[3], v.shape[4]
    L = min(CHUNK, T)
    nc = -(-T // L)
    pad = nc * L - T
    q = q.astype(jnp.float32)
    k = k.astype(jnp.float32)
    v = v.astype(jnp.float32)
    log_a = log_a.astype(jnp.float32)
    if pad:
        q = jnp.pad(q, ((0, 0), (0, pad), (0, 0), (0, 0)))
        k = jnp.pad(k, ((0, 0), (0, pad), (0, 0), (0, 0)))
        v = jnp.pad(v, ((0, 0), (0, pad), (0, 0), (0, 0), (0, 0)))
        log_a = jnp.pad(log_a, ((0, 0), (0, pad), (0, 0), (0, 0)))
    q = q.reshape(b, nc, L, G, N)
    k = k.reshape(b, nc, L, G, N)
    v = v.reshape(b, nc, L, G, R, P)
    log_a = log_a.reshape(b, nc, L, G, R)
    cum = jnp.cumsum(log_a, axis=2)
    cum_t = jnp.moveaxis(cum, 2, -1)
    seg = cum_t[..., :, None] - cum_t[..., None, :]
    causal = jnp.tril(jnp.ones((L, L), dtype=bool))
    decay = jnp.exp(jnp.where(causal, seg, -jnp.inf))
    scores = jnp.einsum('bclgn,bcsgn->bcgls', q, k)
    y_intra = jnp.einsum('bcgrls,bcsgrp->bclgrp', scores[:, :, :, None] * decay, v)
    cum_last = cum[:, :, -1]
    to_end = jnp.exp(cum_last[:, :, None] - cum)
    chunk_states = jnp.einsum('bclgn,bclgr,bclgrp->bcgrpn', k, to_end, v)

    def step(S, inp):
        cs, da = inp
        return da[..., None, None] * S + cs, S

    s_final, s_prev = lax.scan(
        step, s0.astype(jnp.float32),
        (jnp.moveaxis(chunk_states, 1, 0), jnp.moveaxis(jnp.exp(cum_last), 1, 0)))
    s_prev = jnp.moveaxis(s_prev, 0, 1)
    y_inter = jnp.einsum('bclgn,bcgrpn,bclgr->bclgrp', q, s_prev, jnp.exp(cum))
    y = (y_intra + y_inter).reshape(b, nc * L, G, R, P)[:, :T]
    return y, s_final.astype(s0.dtype)


def _causal_conv(xbc, buf, conv_w, conv_b):
    T = xbc.shape[1]
    xpad = jnp.concatenate([buf.astype(xbc.dtype), xbc], axis=1)
    out = conv_b[None, None, :]
    for j in range(CONV_K):
        out = out + conv_w[j][None, None, :] * xpad[:, j:j + T]
    return jax.nn.silu(out), xpad[:, -(CONV_K - 1):]


def _hybrid_mixer(h, positions, conv_buf, ssd_s0, ret_s0, w_in, conv_w, conv_b, dt_bias,
                  a_log, d_skip, ssd_norm_w, ret_norm_w, w_out):
    b, T, _ = h.shape
    proj = h @ w_in
    cuts = np.cumsum([SSD_WIDTH, CONV_DIM, SSD_HEADS, RET_QK_WIDTH, RET_QK_WIDTH, RET_WIDTH]).tolist()
    z, xbc, dt_raw, q, k, v, g = jnp.split(proj, cuts, axis=-1)

    xbc, new_conv = _causal_conv(xbc, conv_buf, conv_w, conv_b)
    xs = xbc[..., :SSD_WIDTH]
    B = xbc[..., SSD_WIDTH:SSD_WIDTH + SSD_GROUPS * SSD_STATE].reshape(b, T, SSD_GROUPS, SSD_STATE)
    C = xbc[..., SSD_WIDTH + SSD_GROUPS * SSD_STATE:].reshape(b, T, SSD_GROUPS, SSD_STATE)
    dt = jax.nn.softplus(dt_raw.astype(jnp.float32) + dt_bias.astype(jnp.float32))
    dt = dt.reshape(b, T, SSD_GROUPS, SSD_HEADS_PER_GROUP)
    A = -jnp.exp(a_log.astype(jnp.float32)).reshape(SSD_GROUPS, SSD_HEADS_PER_GROUP)
    xh = xs.astype(jnp.float32).reshape(b, T, SSD_GROUPS, SSD_HEADS_PER_GROUP, SSD_HEADDIM)
    y_ssd, ssd_state = _chunked_decay_scan(
        C, B, xh * dt[..., None], dt * A,
        ssd_s0.reshape(b, SSD_GROUPS, SSD_HEADS_PER_GROUP, SSD_HEADDIM, SSD_STATE))
    y_ssd = y_ssd + d_skip.astype(jnp.float32).reshape(SSD_GROUPS, SSD_HEADS_PER_GROUP)[:, :, None] * xh
    y_ssd = y_ssd.reshape(b, T, SSD_WIDTH) * jax.nn.silu(z.astype(jnp.float32))
    yg = y_ssd.reshape(b, T, SSD_NORM_GROUPS, SSD_WIDTH // SSD_NORM_GROUPS)
    yg = yg * lax.rsqrt(jnp.mean(yg * yg, axis=-1, keepdims=True) + EPS)
    y_ssd = yg.reshape(b, T, SSD_WIDTH) * ssd_norm_w.astype(jnp.float32)

    qr = _rotary(q.reshape(b, T, RET_HEADS, RET_DK), positions)
    kr = _rotary(k.reshape(b, T, RET_HEADS, RET_DK), positions) * (RET_DK ** -0.5)
    vr = v.reshape(b, T, RET_HEADS, 1, RET_DV)
    log_g = jnp.broadcast_to(_retention_log_decay()[None, None, :, None], (b, T, RET_HEADS, 1))
    y_ret, ret_state = _chunked_decay_scan(
        qr, kr, vr, log_g, ret_s0.reshape(b, RET_HEADS, 1, RET_DV, RET_DK))
    y_ret = y_ret.reshape(b, T, RET_HEADS, RET_DV)
    y_ret = y_ret * lax.rsqrt(jnp.mean(y_ret * y_ret, axis=-1, keepdims=True) + EPS)
    y_ret = y_ret * ret_norm_w.astype(jnp.float32).reshape(RET_HEADS, RET_DV)
    y_ret = y_ret.reshape(b, T, RET_WIDTH) * jax.nn.silu(g.astype(jnp.float32))

    mixed = jnp.concatenate([y_ssd, y_ret], axis=-1).astype(h.dtype)
    out = mixed @ w_out
    return (out, new_conv,
            ssd_state.reshape(b, SSD_HEADS, SSD_HEADDIM, SSD_STATE),
            ret_state.reshape(b, RET_HEADS, RET_DV, RET_DK))


def _trunk(x, positions, conv_bufs, ssd_states, ret_states, norm_w, w_in, conv_w, conv_b,
           dt_bias, a_log, d_skip, ssd_norm_w, ret_norm_w, w_out, final_norm_w):
    new_conv, new_ssd, new_ret = [], [], []
    for l in range(DEPTH):
        h = _rms_norm(x, norm_w[l])
        out, c, s, r = _hybrid_mixer(h, positions, conv_bufs[l], ssd_states[l], ret_states[l],
                                     w_in[l], conv_w[l], conv_b[l], dt_bias[l], a_log[l],
                                     d_skip[l], ssd_norm_w[l], ret_norm_w[l], w_out[l])
        x = x + out.astype(x.dtype)
        new_conv.append(c)
        new_ssd.append(s)
        new_ret.append(r)
    return (_rms_norm(x, final_norm_w), jnp.stack(new_conv), jnp.stack(new_ssd), jnp.stack(new_ret))


def setup_inputs(seed: int = 0) -> dict:
    key = jax.random.key(seed)
    ks = jax.random.split(key, 20)
    f32 = jnp.float32
    x_prompt = jax.random.normal(ks[0], (BATCH, SEQ, D_MODEL), f32)
    x_sample = jax.random.normal(ks[1], (DEC_BATCH, DEC_SEQ, D_MODEL), f32)
    state_conv = jax.random.normal(ks[2], (DEPTH, DEC_BATCH, CONV_K - 1, CONV_DIM), f32)
    state_ssd = 0.5 * jax.random.normal(ks[3], (DEPTH, DEC_BATCH, SSD_HEADS, SSD_HEADDIM, SSD_STATE), f32)
    state_ret = jax.random.normal(ks[4], (DEPTH, DEC_BATCH, RET_HEADS, RET_DV, RET_DK), f32)
    norm_w = 1.0 + 0.02 * jax.random.normal(ks[5], (DEPTH, D_MODEL), f32)
    w_in = jax.random.normal(ks[6], (DEPTH, D_MODEL, IN_COLS), f32) * (D_MODEL ** -0.5)
    conv_w = jax.random.normal(ks[7], (DEPTH, CONV_K, CONV_DIM), f32) * (CONV_K ** -0.5)
    conv_b = 0.01 * jax.random.normal(ks[8], (DEPTH, CONV_DIM), f32)
    dt0 = jnp.exp(jax.random.uniform(ks[9], (DEPTH, SSD_HEADS), f32,
                                     minval=math.log(1e-3), maxval=math.log(1e-1)))
    dt_bias = dt0 + jnp.log(-jnp.expm1(-dt0))
    a_log = jnp.log(jax.random.uniform(ks[10], (DEPTH, SSD_HEADS), f32, minval=1.0, maxval=16.0))
    d_skip = 1.0 + 0.1 * jax.random.normal(ks[11], (DEPTH, SSD_HEADS), f32)
    ssd_norm_w = 1.0 + 0.02 * jax.random.normal(ks[12], (DEPTH, SSD_WIDTH), f32)
    ret_norm_w = 1.0 + 0.02 * jax.random.normal(ks[13], (DEPTH, RET_WIDTH), f32)
    w_out = jax.random.normal(ks[14], (DEPTH, MIX_WIDTH, D_MODEL), f32) * (MIX_WIDTH ** -0.5)
    final_norm_w = 1.0 + 0.02 * jax.random.normal(ks[15], (D_MODEL,), f32)
    return {"x_prompt": x_prompt, "x_sample": x_sample, "state_conv": state_conv,
            "state_ssd": state_ssd, "state_ret": state_ret, "norm_w": norm_w, "w_in": w_in,
            "conv_w": conv_w, "conv_b": conv_b, "dt_bias": dt_bias, "a_log": a_log,
            "d_skip": d_skip, "ssd_norm_w": ssd_norm_w, "ret_norm_w": ret_norm_w,
            "w_out": w_out, "final_norm_w": final_norm_w}


def reference(x_prompt, x_sample, state_conv, state_ssd, state_ret, norm_w, w_in, conv_w, conv_b,
              dt_bias, a_log, d_skip, ssd_norm_w, ret_norm_w, w_out, final_norm_w):
    bp, tp, _ = x_prompt.shape
    bs, ts, _ = x_sample.shape
    dt = x_prompt.dtype
    zc = jnp.zeros((DEPTH, bp, CONV_K - 1, CONV_DIM), dt)
    zs = jnp.zeros((DEPTH, bp, SSD_HEADS, SSD_HEADDIM, SSD_STATE), dt)
    zr = jnp.zeros((DEPTH, bp, RET_HEADS, RET_DV, RET_DK), dt)
    y_prompt, conv_p, ssd_p, ret_p = _trunk(
        x_prompt, jnp.arange(tp), zc, zs, zr, norm_w, w_in, conv_w, conv_b, dt_bias, a_log,
        d_skip, ssd_norm_w, ret_norm_w, w_out, final_norm_w)
    y_sample, conv_s, ssd_s, ret_s = _trunk(
        x_sample, PAST_LEN + jnp.arange(ts), state_conv, state_ssd, state_ret, norm_w, w_in,
        conv_w, conv_b, dt_bias, a_log, d_skip, ssd_norm_w, ret_norm_w, w_out, final_norm_w)
    return (y_prompt, y_sample, conv_p, ssd_p, ret_p, conv_s, ssd_s, ret_s)
```

```python
import functools
import math

import numpy as np
import jax
import jax.numpy as jnp
from jax import lax
from jax.experimental import pallas as pl
from jax.experimental.pallas import tpu as pltpu

SSD_HEADDIM = 64
SSD_STATE = 128
CONV_K = 4
CHUNK = 128
ROPE_BASE = 10000.0
EPS = 1e-5
PAST_LEN = 16384

LANES = 128
SUBLANES = 8
VMEM_LIMIT_BYTES = 56 * 1024 * 1024

_F32 = jnp.float32
_BF16 = jnp.bfloat16
_NT = (((1,), (1,)), ((), ()))
_TN = (((0,), (0,)), ((), ()))


def _cparams(*sem):
    return pltpu.CompilerParams(dimension_semantics=sem, vmem_limit_bytes=VMEM_LIMIT_BYTES)


def _sigmoid(x):
    return 1.0 / (1.0 + jnp.exp(-x))


def _silu(x):
    return x * _sigmoid(x)


def _softplus(x):
    return jnp.maximum(x, 0.0) + jnp.log1p(jnp.exp(-jnp.abs(x)))


def _split2(x):
    hi = x.astype(_BF16)
    lo = (x - hi.astype(_F32)).astype(_BF16)
    return hi, lo


def _split3(x):
    hi = x.astype(_BF16)
    r = x - hi.astype(_F32)
    mid = r.astype(_BF16)
    lo = (r - mid.astype(_F32)).astype(_BF16)
    return hi, mid, lo


class Dims:
    def __init__(self, d_model, n_heads, conv_dim, ret_heads, ret_dv, ret_dk):
        self.D = d_model
        self.W = d_model
        self.H = n_heads
        self.P = SSD_HEADDIM
        self.N = SSD_STATE
        self.GN = (conv_dim - self.W) // 2
        self.G = self.GN // self.N
        self.HPG = self.H // self.G
        self.GW = self.HPG * self.P
        self.CD = conv_dim
        self.RH = ret_heads
        self.DV = ret_dv
        self.DK = ret_dk
        self.QK = ret_heads * ret_dk
        self.o_z = 0
        self.o_xs = self.W
        self.o_v = 2 * self.W
        self.o_g = 3 * self.W
        self.o_b = 4 * self.W
        self.o_c = self.o_b + self.GN
        self.o_q = self.o_c + self.GN
        self.o_k = self.o_q + self.QK
        self.o_dt = self.o_k + self.QK
        used = self.o_dt + LANES
        self.TN = 512
        self.NP = -(-used // self.TN) * self.TN
        assert self.H <= LANES and self.P * 2 == LANES and self.N == LANES
        assert self.DK == LANES and self.DV == 2 * LANES
        assert self.GW % LANES == 0 and self.W % self.GW == 0
        for off, width in ((self.o_b, self.GN), (self.o_c, self.GN), (self.o_q, self.QK),
                           (self.o_k, self.QK), (self.o_dt, LANES)):
            assert off % width == 0


def _norm_kernel(x_ref, w_ref, o_ref):
    x = x_ref[...]
    ms = jnp.mean(x * x, axis=-1, keepdims=True)
    o_ref[...] = (x * lax.rsqrt(ms + EPS) * w_ref[...]).astype(o_ref.dtype)


def _rmsnorm(x, w, out_dtype, tm):
    m, d = x.shape
    return pl.pallas_call(
        _norm_kernel,
        grid=(m // tm,),
        in_specs=[pl.BlockSpec((tm, d), lambda i: (i, 0)),
                  pl.BlockSpec((1, d), lambda i: (0, 0))],
        out_specs=pl.BlockSpec((tm, d), lambda i: (i, 0)),
        out_shape=jax.ShapeDtypeStruct((m, d), out_dtype),
        compiler_params=_cparams("parallel"),
        name="rmsnorm",
    )(x, w.reshape(1, d))


def _inproj_kernel(a_ref, w_ref, o_ref):
    o_ref[...] = jnp.dot(a_ref[...], w_ref[...], preferred_element_type=_F32)


def _inproj(a, w, tm, tn):
    m, k = a.shape
    n = w.shape[1]
    return pl.pallas_call(
        _inproj_kernel,
        grid=(m // tm, n // tn),
        in_specs=[pl.BlockSpec((tm, k), lambda i, j: (i, 0)),
                  pl.BlockSpec((k, tn), lambda i, j: (0, j))],
        out_specs=pl.BlockSpec((tm, tn), lambda i, j: (i, j)),
        out_shape=jax.ShapeDtypeStruct((m, n), _F32),
        compiler_params=_cparams("parallel", "arbitrary"),
        name="inproj",
    )(a, w)


def _outproj_kernel(a1_ref, a2_ref, w1_ref, w2_ref, x_ref, o_ref):
    acc = jnp.dot(a1_ref[...], w1_ref[...], preferred_element_type=_F32)
    acc = acc + jnp.dot(a2_ref[...], w2_ref[...], preferred_element_type=_F32)
    o_ref[...] = x_ref[...] + acc


def _outproj(a1, a2, w, x, tm, tn):
    m, k = a1.shape
    n = w.shape[1]
    return pl.pallas_call(
        _outproj_kernel,
        grid=(m // tm, n // tn),
        in_specs=[pl.BlockSpec((tm, k), lambda i, j: (i, 0)),
                  pl.BlockSpec((tm, k), lambda i, j: (i, 0)),
                  pl.BlockSpec((k, tn), lambda i, j: (0, j)),
                  pl.BlockSpec((k, tn), lambda i, j: (1, j)),
                  pl.BlockSpec((tm, tn), lambda i, j: (i, j))],
        out_specs=pl.BlockSpec((tm, tn), lambda i, j: (i, j)),
        out_shape=jax.ShapeDtypeStruct((m, n), _F32),
        compiler_params=_cparams("parallel", "arbitrary"),
        name="outproj",
    )(a1, a2, w, w, x)


def _ssd_prompt_kernel(dm, z_ref, xs_ref, b_ref, c_ref, dt_ref, cw_ref, cb_ref, dtb_ref, alog_ref,
                       dskip_ref, nw_ref, e_ref, y_ref, conv_ref, st_out_ref, xp_ref, xc_ref, st_ref):
    L, W, GN, GW, N = CHUNK, dm.W, dm.GN, dm.GW, dm.N
    c = pl.program_id(1)
    nc = pl.num_programs(1)

    @pl.when(c == 0)
    def _():
        xp_ref[0:SUBLANES, :] = jnp.zeros((SUBLANES, dm.CD), _F32)
        st_ref[...] = jnp.zeros_like(st_ref)

    xp_ref[SUBLANES:SUBLANES + L, 0:W] = xs_ref[...]
    xp_ref[SUBLANES:SUBLANES + L, W:W + GN] = b_ref[...]
    xp_ref[SUBLANES:SUBLANES + L, W + GN:W + 2 * GN] = c_ref[...]

    slab = 512
    for s in range(dm.CD // slab):
        sl = slice(s * slab, (s + 1) * slab)
        acc = cb_ref[:, sl]
        for j in range(CONV_K):
            r0 = SUBLANES - (CONV_K - 1) + j
            acc = acc + cw_ref[j:j + 1, sl] * xp_ref[r0:r0 + L, sl]
        xc_ref[:, sl] = _silu(acc)

    tail = xp_ref[SUBLANES + L - (CONV_K - 1):SUBLANES + L, :]
    conv_ref[0] = tail
    xp_ref[SUBLANES - (CONV_K - 1):SUBLANES, :] = tail

    dt = _softplus(dt_ref[...] + dtb_ref[...])
    a_neg = -jnp.exp(alog_ref[...])
    la = dt * a_neg
    row = lax.broadcasted_iota(jnp.int32, (L, L), 0)
    col = lax.broadcasted_iota(jnp.int32, (L, L), 1)
    causal = row >= col
    upper = (row <= col).astype(_BF16)
    cum_t = None
    for part in _split3(la.T):
        d = jnp.dot(part, upper, preferred_element_type=_F32)
        cum_t = d if cum_t is None else cum_t + d
    cum = cum_t.T
    ecum = jnp.exp(cum)
    to_end = jnp.exp(cum[L - 1:L, :] - cum)
    stacked = jnp.concatenate([dt, ecum, to_end], axis=0)
    s_hi, s_lo = _split2(stacked)
    lane = lax.broadcasted_iota(jnp.int32, (L, LANES), 1)
    low_half = lane < dm.P

    for g in range(dm.G):
        sl = slice(g * GW, (g + 1) * GW)
        e_g = e_ref[:, sl]
        ex = (jnp.dot(s_hi, e_g, preferred_element_type=_F32)
              + jnp.dot(s_lo, e_g, preferred_element_type=_F32))
        dt_e, ecum_e, toe_e = ex[0:L], ex[L:2 * L], ex[2 * L:3 * L]
        xs_g = xc_ref[:, sl]
        xdt = xs_g * dt_e
        b_g = xc_ref[:, W + g * N:W + (g + 1) * N]
        c_g = xc_ref[:, W + GN + g * N:W + GN + (g + 1) * N]
        b_bf = b_g.astype(_BF16)
        c_bf = c_g.astype(_BF16)
        scores = lax.dot_general(c_bf, b_bf, _NT, preferred_element_type=_F32)
        st_g = st_ref[:, sl]
        y_parts = []
        for j in range(GW // LANES):
            xpair = xdt[:, j * LANES:(j + 1) * LANES].astype(_BF16)
            acc = None
            for which in range(2):
                h = g * dm.HPG + 2 * j + which
                seg = cum[:, h:h + 1] - cum_t[h:h + 1, :]
                decay = jnp.exp(jnp.where(causal, seg, -jnp.inf))
                m_h = (scores * decay).astype(_BF16)
                keep = low_half if which == 0 else jnp.logical_not(low_half)
                x_h = jnp.where(keep, xpair, jnp.zeros_like(xpair))
                d = jnp.dot(m_h, x_h, preferred_element_type=_F32)
                acc = d if acc is None else acc + d
            y_parts.append(acc)
        y = jnp.concatenate(y_parts, axis=1)
        y = y + jnp.dot(c_bf, st_g.astype(_BF16), preferred_element_type=_F32) * ecum_e
        w_g = (xdt * toe_e).astype(_BF16)
        cs = lax.dot_general(b_bf, w_g, _TN, preferred_element_type=_F32)
        st_ref[:, sl] = st_g * ecum_e[L - 1:L, :] + cs
        y = y + dskip_ref[:, sl] * xs_g
        y = y * _silu(z_ref[:, sl])
        ms = jnp.mean(y * y, axis=-1, keepdims=True)
        y_ref[:, sl] = (y * lax.rsqrt(ms + EPS) * nw_ref[:, sl]).astype(y_ref.dtype)

    @pl.when(c == nc - 1)
    def _():
        for j in range(W // LANES):
            st_out_ref[0, j * LANES:(j + 1) * LANES, :] = st_ref[:, j * LANES:(j + 1) * LANES].T


def _ssd_prompt(dm, proj, nb, conv_w, conv_b, dtb, alog, dskip_e, nw, e_mat):
    L, W, GN = CHUNK, dm.W, dm.GN
    m = proj.shape[0]
    nc = m // nb // L
    rowblk = lambda b, c: b * nc + c
    cst = lambda b, c: (0, 0)
    in_specs = [
        pl.BlockSpec((L, W), lambda b, c: (rowblk(b, c), dm.o_z // W)),
        pl.BlockSpec((L, W), lambda b, c: (rowblk(b, c), dm.o_xs // W)),
        pl.BlockSpec((L, GN), lambda b, c: (rowblk(b, c), dm.o_b // GN)),
        pl.BlockSpec((L, GN), lambda b, c: (rowblk(b, c), dm.o_c // GN)),
        pl.BlockSpec((L, LANES), lambda b, c: (rowblk(b, c), dm.o_dt // LANES)),
        pl.BlockSpec((CONV_K, dm.CD), cst),
        pl.BlockSpec((1, dm.CD), cst),
        pl.BlockSpec((1, LANES), cst),
        pl.BlockSpec((1, LANES), cst),
        pl.BlockSpec((1, W), cst),
        pl.BlockSpec((1, W), cst),
        pl.BlockSpec((LANES, W), cst),
    ]
    out_specs = [
        pl.BlockSpec((L, W), lambda b, c: (rowblk(b, c), 0)),
        pl.BlockSpec((1, CONV_K - 1, dm.CD), lambda b, c: (b, 0, 0)),
        pl.BlockSpec((1, W, dm.N), lambda b, c: (b, 0, 0)),
    ]
    out_shape = [
        jax.ShapeDtypeStruct((m, W), _BF16),
        jax.ShapeDtypeStruct((nb, CONV_K - 1, dm.CD), _F32),
        jax.ShapeDtypeStruct((nb, W, dm.N), _F32),
    ]
    return pl.pallas_call(
        functools.partial(_ssd_prompt_kernel, dm),
        grid=(nb, nc),
        in_specs=in_specs,
        out_specs=out_specs,
        out_shape=out_shape,
        scratch_shapes=[pltpu.VMEM((SUBLANES + L, dm.CD), _F32),
                        pltpu.VMEM((L, dm.CD), _F32),
                        pltpu.VMEM((dm.N, W), _F32)],
        compiler_params=_cparams("arbitrary", "arbitrary"),
        name="ssd_prompt",
    )(proj, proj, proj, proj, proj, conv_w, conv_b, dtb, alog, dskip_e, nw, e_mat)


def _ret_log_gammas(n):
    return [float(np.log1p(-np.exp2(np.float32(-5.0 - h)))) for h in range(n)]


def _rotate(x, cos2, sin2):
    return x * cos2 + pltpu.roll(x, LANES // 2, axis=1) * sin2


def _ret_prompt_kernel(dm, q_ref, k_ref, v_ref, g_ref, cos_ref, sin_ref, nw_ref,
                       y_ref, st_out_ref, dec_ref, ec_ref, te_ref, st_ref):
    L, DK, DV = CHUNK, dm.DK, dm.DV
    b = pl.program_id(0)
    c = pl.program_id(1)
    nc = pl.num_programs(1)
    lgs = _ret_log_gammas(dm.RH)

    @pl.when(jnp.logical_and(b == 0, c == 0))
    def _():
        row = lax.broadcasted_iota(jnp.int32, (L, L), 0)
        col = lax.broadcasted_iota(jnp.int32, (L, L), 1)
        diff = (row - col).astype(_F32)
        rowf = row.astype(_F32)
        for h in range(dm.RH):
            dec_ref[h] = jnp.exp(jnp.where(row >= col, diff * lgs[h], -jnp.inf))
            ec_ref[h] = jnp.exp((rowf + 1.0) * lgs[h])
            te_ref[h] = jnp.exp((float(L - 1) - rowf) * lgs[h])

    @pl.when(c == 0)
    def _():
        st_ref[...] = jnp.zeros_like(st_ref)

    cos2 = cos_ref[...]
    sin2 = sin_ref[...]
    scale = DK ** -0.5
    for h in range(dm.RH):
        qs = slice(h * DK, (h + 1) * DK)
        vs = slice(h * DV, (h + 1) * DV)
        qr = _rotate(q_ref[:, qs], cos2, sin2)
        kr = _rotate(k_ref[:, qs], cos2, sin2) * scale
        q_bf = qr.astype(_BF16)
        k_bf = kr.astype(_BF16)
        v_h = v_ref[:, vs]
        scores = lax.dot_general(q_bf, k_bf, _NT, preferred_element_type=_F32)
        m_h = (scores * dec_ref[h]).astype(_BF16)
        st_h = st_ref[h]
        ec = ec_ref[h]
        te = te_ref[h]
        y = jnp.dot(m_h, v_h.astype(_BF16), preferred_element_type=_F32)
        y_inter = jnp.dot(q_bf, st_h.astype(_BF16), preferred_element_type=_F32)
        y = y + y_inter * jnp.concatenate([ec, ec], axis=1)
        vt = (v_h * jnp.concatenate([te, te], axis=1)).astype(_BF16)
        cs = lax.dot_general(k_bf, vt, _TN, preferred_element_type=_F32)
        st_ref[h] = st_h * ec[L - 1:L, 0:1] + cs
        ms = jnp.mean(y * y, axis=-1, keepdims=True)
        y = y * lax.rsqrt(ms + EPS) * nw_ref[:, vs] * _silu(g_ref[:, vs])
        y_ref[:, vs] = y.astype(y_ref.dtype)

    @pl.when(c == nc - 1)
    def _():
        for h in range(dm.RH):
            st_out_ref[0, h * DV:(h + 1) * DV, :] = st_ref[h].T


def _ret_prompt(dm, proj, nb, cos2, sin2, nw):
    L, W, QK = CHUNK, dm.W, dm.QK
    m = proj.shape[0]
    nc = m // nb // L
    rowblk = lambda b, c: b * nc + c
    in_specs = [
        pl.BlockSpec((L, QK), lambda b, c: (rowblk(b, c), dm.o_q // QK)),
        pl.BlockSpec((L, QK), lambda b, c: (rowblk(b, c), dm.o_k // QK)),
        pl.BlockSpec((L, W), lambda b, c: (rowblk(b, c), dm.o_v // W)),
        pl.BlockSpec((L, W), lambda b, c: (rowblk(b, c), dm.o_g // W)),
        pl.BlockSpec((L, LANES), lambda b, c: (c, 0)),
        pl.BlockSpec((L, LANES), lambda b, c: (c, 0)),
        pl.BlockSpec((1, W), lambda b, c: (0, 0)),
    ]
    out_specs = [
        pl.BlockSpec((L, W), lambda b, c: (rowblk(b, c), 0)),
        pl.BlockSpec((1, dm.RH * dm.DV, dm.DK), lambda b, c: (b, 0, 0)),
    ]
    out_shape = [
        jax.ShapeDtypeStruct((m, W), _BF16),
        jax.ShapeDtypeStruct((nb, dm.RH * dm.DV, dm.DK), _F32),
    ]
    return pl.pallas_call(
        functools.partial(_ret_prompt_kernel, dm),
        grid=(nb, nc),
        in_specs=in_specs,
        out_specs=out_specs,
        out_shape=out_shape,
        scratch_shapes=[pltpu.VMEM((dm.RH, L, L), _F32),
                        pltpu.VMEM((dm.RH, L, LANES), _F32),
                        pltpu.VMEM((dm.RH, L, LANES), _F32),
                        pltpu.VMEM((dm.RH, dm.DK, dm.DV), _F32)],
        compiler_params=_cparams("arbitrary", "arbitrary"),
        name="ret_prompt",
    )(proj, proj, proj, proj, cos2, sin2, nw)


def _row0(x, rows=SUBLANES):
    r = lax.broadcasted_iota(jnp.int32, (rows, x.shape[1]), 0)
    return jnp.where(r == 0, jnp.broadcast_to(x, (rows, x.shape[1])), jnp.zeros((), x.dtype))


def _ssd_step_kernel(dm, z_ref, xs_ref, b_ref, c_ref, dt_ref, cst_ref, st_ref, cw_ref, cb_ref,
                     dtb_ref, alog_ref, dskip_ref, nw_ref, e_ref, *rest):
    y_ref, conv_ref, st_out_ref = rest[-3:]
    W, GN, GW, N = dm.W, dm.GN, dm.GW, dm.N
    xbc = jnp.concatenate([xs_ref[0], b_ref[0], c_ref[0]], axis=1)
    cst = cst_ref[0, 0]
    acc = cb_ref[...]
    for j in range(CONV_K - 1):
        acc = acc + cw_ref[j:j + 1, :] * cst[j:j + 1, :]
    acc = acc + cw_ref[CONV_K - 1:CONV_K, :] * xbc
    xc = _silu(acc)
    conv_ref[0] = jnp.concatenate([cst[1:CONV_K - 1, :], xbc], axis=0)

    dt = _softplus(dt_ref[0] + dtb_ref[...])
    decay = jnp.exp(dt * (-jnp.exp(alog_ref[...])))
    r8 = lax.broadcasted_iota(jnp.int32, (SUBLANES, LANES), 0)
    stacked = jnp.where(r8 == 0, jnp.broadcast_to(dt, (SUBLANES, LANES)),
                        jnp.where(r8 == 1, jnp.broadcast_to(decay, (SUBLANES, LANES)), 0.0))
    s_hi, s_lo = _split2(stacked)
    ones_n = jnp.ones((1, N), _F32)

    for g in range(dm.G):
        sl = slice(g * GW, (g + 1) * GW)
        e_g = e_ref[:, sl]
        ex = (jnp.dot(s_hi, e_g, preferred_element_type=_F32)
              + jnp.dot(s_lo, e_g, preferred_element_type=_F32))
        dt_e, dec_e = ex[0:1], ex[1:2]
        xs_g = xc[:, sl]
        xdt = xs_g * dt_e
        b_g = xc[:, W + g * N:W + (g + 1) * N]
        c_g = xc[:, W + GN + g * N:W + GN + (g + 1) * N]
        cb_dot = jnp.sum(c_g * b_g, axis=-1, keepdims=True)
        s_old = st_ref[0, 0, g * GW:(g + 1) * GW, :]
        y_old = lax.dot_general(_row0(c_g).astype(_BF16), s_old.astype(_BF16), _NT,
                                preferred_element_type=_F32)[0:1]
        y = dec_e * y_old + cb_dot * xdt
        d_hi = dec_e.astype(_BF16).astype(_F32)
        d_lo = dec_e - d_hi
        r8w = lax.broadcasted_iota(jnp.int32, (SUBLANES, GW), 0)
        lhs = jnp.where(r8w == 0, jnp.broadcast_to(d_hi, (SUBLANES, GW)),
                        jnp.where(r8w == 1, jnp.broadcast_to(d_lo, (SUBLANES, GW)),
                                  jnp.where(r8w == 2, jnp.broadcast_to(xdt, (SUBLANES, GW)),
                                            0.0))).astype(_BF16)
        r8n = lax.broadcasted_iota(jnp.int32, (SUBLANES, N), 0)
        rhs_a = jnp.where(r8n < 2, jnp.broadcast_to(ones_n, (SUBLANES, N)), 0.0)
        rhs_u = jnp.where(r8n == 2, jnp.broadcast_to(b_g, (SUBLANES, N)), 0.0)
        rhs = jnp.concatenate([rhs_a, rhs_u], axis=1).astype(_BF16)
        res = lax.dot_general(lhs, rhs, _TN, preferred_element_type=_F32)
        st_out_ref[0, 0, g * GW:(g + 1) * GW, :] = s_old * res[:, 0:N] + res[:, N:2 * N]
        y = y + dskip_ref[:, sl] * xs_g
        y = y * _silu(z_ref[0][:, sl])
        ms = jnp.mean(y * y, axis=-1, keepdims=True)
        y_ref[0, :, sl] = (y * lax.rsqrt(ms + EPS) * nw_ref[:, sl]).astype(y_ref.dtype)


def _ssd_step(dm, layer, proj3, state_conv, state_ssd, prev_out, conv_w, conv_b, dtb, alog,
              dskip_e, nw, e_mat):
    W, GN = dm.W, dm.GN
    nb = proj3.shape[0]
    cst = lambda b: (0, 0)
    in_specs = [
        pl.BlockSpec((1, 1, W), lambda b: (b, 0, dm.o_z // W)),
        pl.BlockSpec((1, 1, W), lambda b: (b, 0, dm.o_xs // W)),
        pl.BlockSpec((1, 1, GN), lambda b: (b, 0, dm.o_b // GN)),
        pl.BlockSpec((1, 1, GN), lambda b: (b, 0, dm.o_c // GN)),
        pl.BlockSpec((1, 1, LANES), lambda b: (b, 0, dm.o_dt // LANES)),
        pl.BlockSpec((1, 1, CONV_K - 1, dm.CD), lambda b: (layer, b, 0, 0)),
        pl.BlockSpec((1, 1, W, dm.N), lambda b: (layer, b, 0, 0)),
        pl.BlockSpec((CONV_K, dm.CD), cst),
        pl.BlockSpec((1, dm.CD), cst),
        pl.BlockSpec((1, LANES), cst),
        pl.BlockSpec((1, LANES), cst),
        pl.BlockSpec((1, W), cst),
        pl.BlockSpec((1, W), cst),
        pl.BlockSpec((LANES, W), cst),
    ]
    args = [proj3, proj3, proj3, proj3, proj3, state_conv, state_ssd, conv_w, conv_b, dtb, alog,
            dskip_e, nw, e_mat]
    aliases = {}
    if prev_out is not None:
        in_specs.append(pl.BlockSpec(memory_space=pl.ANY))
        args.append(prev_out)
        aliases = {len(args) - 1: 2}
    out_specs = [
        pl.BlockSpec((1, 1, W), lambda b: (b, 0, 0)),
        pl.BlockSpec((1, CONV_K - 1, dm.CD), lambda b: (b, 0, 0)),
        pl.BlockSpec((1, 1, W, dm.N), lambda b: (layer, b, 0, 0)),
    ]
    out_shape = [
        jax.ShapeDtypeStruct((nb, 1, W), _BF16),
        jax.ShapeDtypeStruct((nb, CONV_K - 1, dm.CD), _F32),
        jax.ShapeDtypeStruct(state_ssd.shape, _F32),
    ]
    return pl.pallas_call(
        functools.partial(_ssd_step_kernel, dm),
        grid=(nb,),
        in_specs=in_specs,
        out_specs=out_specs,
        out_shape=out_shape,
        input_output_aliases=aliases,
        compiler_params=_cparams("arbitrary"),
        name="ssd_step",
    )(*args)


def _ret_step_kernel(dm, q_ref, k_ref, v_ref, g_ref, cos_ref, sin_ref, st_ref, nw_ref, *rest):
    y_ref, st_out_ref = rest[-2:]
    DK, DV = dm.DK, dm.DV
    lgs = _ret_log_gammas(dm.RH)
    cos2 = cos_ref[...]
    sin2 = sin_ref[...]
    scale = DK ** -0.5
    r8k = lax.broadcasted_iota(jnp.int32, (SUBLANES, DK), 0)
    for h in range(dm.RH):
        gamma = float(np.exp(np.float32(lgs[h])))
        qs = slice(h * DK, (h + 1) * DK)
        vs = slice(h * DV, (h + 1) * DV)
        qr = _rotate(q_ref[0][:, qs], cos2, sin2)
        kr = _rotate(k_ref[0][:, qs], cos2, sin2) * scale
        v_h = v_ref[0][:, vs]
        s_old = st_ref[0, 0, h * DV:(h + 1) * DV, :]
        y_old = lax.dot_general(_row0(qr).astype(_BF16), s_old.astype(_BF16), _NT,
                                preferred_element_type=_F32)[0:1]
        qk = jnp.sum(qr * kr, axis=-1, keepdims=True)
        y = gamma * y_old + qk * v_h
        outer = lax.dot_general(_row0(v_h).astype(_BF16),
                                jnp.where(r8k == 0, jnp.broadcast_to(kr, (SUBLANES, DK)), 0.0).astype(_BF16),
                                _TN, preferred_element_type=_F32)
        st_out_ref[0, 0, h * DV:(h + 1) * DV, :] = gamma * s_old + outer
        ms = jnp.mean(y * y, axis=-1, keepdims=True)
        y = y * lax.rsqrt(ms + EPS) * nw_ref[:, vs] * _silu(g_ref[0][:, vs])
        y_ref[0, :, vs] = y.astype(y_ref.dtype)


def _ret_step(dm, layer, proj3, state_ret, prev_out, cos2, sin2, nw):
    W, QK = dm.W, dm.QK
    nb = proj3.shape[0]
    cst = lambda b: (0, 0)
    in_specs = [
        pl.BlockSpec((1, 1, QK), lambda b: (b, 0, dm.o_q // QK)),
        pl.BlockSpec((1, 1, QK), lambda b: (b, 0, dm.o_k // QK)),
        pl.BlockSpec((1, 1, W), lambda b: (b, 0, dm.o_v // W)),
        pl.BlockSpec((1, 1, W), lambda b: (b, 0, dm.o_g // W)),
        pl.BlockSpec((1, LANES), cst),
        pl.BlockSpec((1, LANES), cst),
        pl.BlockSpec((1, 1, dm.RH * dm.DV, dm.DK), lambda b: (layer, b, 0, 0)),
        pl.BlockSpec((1, W), cst),
    ]
    args = [proj3, proj3, proj3, proj3, cos2, sin2, state_ret, nw]
    aliases = {}
    if prev_out is not None:
        in_specs.append(pl.BlockSpec(memory_space=pl.ANY))
        args.append(prev_out)
        aliases = {len(args) - 1: 1}
    out_specs = [
        pl.BlockSpec((1, 1, W), lambda b: (b, 0, 0)),
        pl.BlockSpec((1, 1, dm.RH * dm.DV, dm.DK), lambda b: (layer, b, 0, 0)),
    ]
    out_shape = [
        jax.ShapeDtypeStruct((nb, 1, W), _BF16),
        jax.ShapeDtypeStruct(state_ret.shape, _F32),
    ]
    return pl.pallas_call(
        functools.partial(_ret_step_kernel, dm),
        grid=(nb,),
        in_specs=in_specs,
        out_specs=out_specs,
        out_shape=out_shape,
        input_output_aliases=aliases,
        compiler_params=_cparams("arbitrary"),
        name="ret_step",
    )(*args)


def _arrange_w_in(dm, w_in):
    W, GN, H, QK = dm.W, dm.GN, dm.H, dm.QK
    o = 0
    pieces = {}
    for name, width in (("z", W), ("xs", W), ("b", GN), ("c", GN), ("dt", H), ("q", QK), ("k", QK),
                        ("v", W), ("g", W)):
        pieces[name] = w_in[..., o:o + width].astype(_BF16)
        o += width
    used = dm.o_dt + H
    pad = jnp.zeros(w_in.shape[:-1] + (dm.NP - used,), _BF16)
    order = ("z", "xs", "v", "g", "b", "c", "q", "k", "dt")
    return jnp.concatenate([pieces[n] for n in order] + [pad], axis=-1)


def _rope_tables(dk, positions):
    half = dk // 2
    inv = 1.0 / (ROPE_BASE ** (jnp.arange(half, dtype=_F32) / half))
    ang = positions.astype(_F32)[:, None] * inv[None, :]
    cos = jnp.cos(ang)
    sin = jnp.sin(ang)
    return jnp.concatenate([cos, cos], axis=1), jnp.concatenate([-sin, sin], axis=1)


def _pad_lanes(v):
    return jnp.pad(v, (0, LANES - v.shape[0])).reshape(1, LANES)


def kernel(x_prompt, x_sample, state_conv, state_ssd, state_ret, norm_w, w_in, conv_w, conv_b, dt_bias,
           a_log, d_skip, ssd_norm_w, ret_norm_w, w_out, final_norm_w):
    bp, tp, d_model = x_prompt.shape
    bs, ts, _ = x_sample.shape
    depth = norm_w.shape[0]
    assert ts == 1 and tp % CHUNK == 0
    dm = Dims(d_model, dt_bias.shape[1], conv_w.shape[2], state_ret.shape[2], state_ret.shape[3],
              state_ret.shape[4])
    W = dm.W

    w_in_p = _arrange_w_in(dm, w_in)
    w_out_b = w_out.astype(_BF16)
    heads = jnp.arange(LANES)[:, None]
    lanes = jnp.arange(W)[None, :]
    e_mat = (lanes // dm.P == heads).astype(_BF16)
    cos_p, sin_p = _rope_tables(dm.DK, jnp.arange(tp))
    cos_s, sin_s = _rope_tables(dm.DK, PAST_LEN + jnp.arange(ts))

    st_ssd_in = state_ssd.reshape(depth, bs, dm.H * dm.P, dm.N)
    st_ret_in = state_ret.reshape(depth, bs, dm.RH * dm.DV, dm.DK)

    xp = x_prompt.reshape(bp * tp, d_model)
    xs = x_sample.reshape(bs * ts, d_model)
    mp = xp.shape[0]
    tm_p = min(1024, mp)
    tm_o = min(512, mp)
    tm_n = min(256, mp)

    conv_p, ssd_p, ret_p, conv_s = [], [], [], []
    ssd_s = None
    ret_s = None
    for l in range(depth):
        cw, cb = conv_w[l], conv_b[l].reshape(1, -1)
        dtb, alog = _pad_lanes(dt_bias[l]), _pad_lanes(a_log[l])
        dskip_e = jnp.repeat(d_skip[l], dm.P).reshape(1, W)
        snw, rnw = ssd_norm_w[l].reshape(1, W), ret_norm_w[l].reshape(1, W)

        h = _rmsnorm(xp, norm_w[l], _BF16, tm_n)
        proj = _inproj(h, w_in_p[l], tm_p, dm.TN)
        y_ssd, c_new, s_new = _ssd_prompt(dm, proj, bp, cw, cb, dtb, alog, dskip_e, snw, e_mat)
        y_ret, r_new = _ret_prompt(dm, proj, bp, cos_p, sin_p, rnw)
        xp = _outproj(y_ssd, y_ret, w_out_b[l], xp, tm_o, dm.TN)
        conv_p.append(c_new)
        ssd_p.append(s_new.reshape(bp, dm.H, dm.P, dm.N))
        ret_p.append(r_new.reshape(bp, dm.RH, dm.DV, dm.DK))

        h = _rmsnorm(xs, norm_w[l], _BF16, bs)
        proj3 = _inproj(h, w_in_p[l], bs, dm.TN).reshape(bs, 1, dm.NP)
        y_ssd, c_new, ssd_s = _ssd_step(dm, l, proj3, state_conv, st_ssd_in, ssd_s, cw, cb, dtb, alog,
                                        dskip_e, snw, e_mat)
        y_ret, ret_s = _ret_step(dm, l, proj3, st_ret_in, ret_s, cos_s, sin_s, rnw)
        xs = _outproj(y_ssd.reshape(bs, W), y_ret.reshape(bs, W), w_out_b[l], xs, bs, dm.TN)
        conv_s.append(c_new)

    y_prompt = _rmsnorm(xp, final_norm_w, _F32, tm_n).reshape(bp, tp, d_model)
    y_sample = _rmsnorm(xs, final_norm_w, _F32, bs).reshape(bs, ts, d_model)
    return (y_prompt, y_sample, jnp.stack(conv_p), jnp.stack(ssd_p), jnp.stack(ret_p),
            jnp.stack(conv_s),
            ssd_s.reshape(depth, bs, dm.H, dm.P, dm.N),
            ret_s.reshape(depth, bs, dm.RH, dm.DV, dm.DK))
```

```python
import functools

import numpy as np
import jax
import jax.numpy as jnp
from jax import lax
from jax.experimental import pallas as pl
from jax.experimental.pallas import tpu as pltpu

SSD_HEADDIM = 64
SSD_STATE = 128
CONV_K = 4
CHUNK = 128
ROPE_BASE = 10000.0
EPS = 1e-5
PAST_LEN = 16384

LANES = 128
SUBLANES = 8
VMEM_LIMIT_BYTES = 56 * 1024 * 1024

_F32 = jnp.float32
_BF16 = jnp.bfloat16
_NT = (((1,), (1,)), ((), ()))
_TN = (((0,), (0,)), ((), ()))


def _cparams(*sem):
    return pltpu.CompilerParams(dimension_semantics=sem, vmem_limit_bytes=VMEM_LIMIT_BYTES)


def _silu(x):
    h = 0.5 * x
    return h + h * jnp.tanh(h)


def _softplus(x):
    return jnp.maximum(x, 0.0) + jnp.log1p(jnp.exp(-jnp.abs(x)))


def _split2(x):
    hi = x.astype(_BF16)
    lo = (x - hi.astype(_F32)).astype(_BF16)
    return hi, lo


def _split3(x):
    hi = x.astype(_BF16)
    r = x - hi.astype(_F32)
    mid = r.astype(_BF16)
    lo = (r - mid.astype(_F32)).astype(_BF16)
    return hi, mid, lo


def _largest_divisor(candidates, *values):
    for c in candidates:
        if all(v % c == 0 for v in values):
            return c
    raise ValueError((candidates, values))


class Dims:
    def __init__(self, d_model, n_heads, conv_dim, ret_heads, ret_dv, ret_dk):
        self.D = d_model
        self.W = d_model
        self.H = n_heads
        self.P = SSD_HEADDIM
        self.N = SSD_STATE
        self.GN = (conv_dim - self.W) // 2
        self.G = self.GN // self.N
        self.HPG = self.H // self.G
        self.GW = self.HPG * self.P
        self.CD = conv_dim
        self.RH = ret_heads
        self.DV = ret_dv
        self.DK = ret_dk
        self.QK = ret_heads * ret_dk
        self.TN = 512
        self.o_dt = 2 * self.W + 2 * self.GN
        self.NA = -(-(self.o_dt + LANES) // self.TN) * self.TN
        self.NB = 2 * self.QK + 2 * self.W
        self.IN_COLS = self.o_dt + self.H + self.NB
        assert self.H < LANES and self.P * 2 == LANES and self.N == LANES
        assert self.DK == LANES and self.DV == 2 * LANES and 2 * self.QK == self.W
        assert self.GW % LANES == 0 and self.W % self.GW == 0
        assert (2 * self.W) % self.GN == 0 and self.NB % self.TN == 0


def _cast_kernel(limit, w_ref, o_ref):
    x = w_ref[...]
    if limit is not None:
        col = pl.program_id(2) * x.shape[1] + lax.broadcasted_iota(jnp.int32, x.shape, 1)
        x = jnp.where(col < limit, x, 0.0)
    o_ref[...] = x.astype(o_ref.dtype)


def _cast_weights(w, n_out, limit, tk, tn):
    depth, k, _ = w.shape
    return pl.pallas_call(
        functools.partial(_cast_kernel, limit),
        grid=(depth, k // tk, n_out // tn),
        in_specs=[pl.BlockSpec((None, tk, tn), lambda l, i, j: (l, i, j))],
        out_specs=pl.BlockSpec((None, tk, tn), lambda l, i, j: (l, i, j)),
        out_shape=jax.ShapeDtypeStruct((depth, k, n_out), _BF16),
        compiler_params=_cparams("parallel", "parallel", "parallel"),
        name="cast_weights",
    )(w)


def _shift_cast_kernel(shift, a_ref, nxt_ref, o_ref):
    full = jnp.concatenate([a_ref[...], nxt_ref[...]], axis=1)
    n = full.shape[1]
    rolled = pltpu.roll(full, n - shift, axis=1)
    o_ref[...] = rolled[:, :o_ref.shape[1]].astype(o_ref.dtype)


def _shift_cast_weights(w, src_off, shift, n_out, tk, wb):
    depth, k, _ = w.shape
    return pl.pallas_call(
        functools.partial(_shift_cast_kernel, shift),
        grid=(depth, k // tk, n_out // wb),
        in_specs=[pl.BlockSpec((None, tk, wb), lambda l, i, j: (l, i, src_off // wb + j)),
                  pl.BlockSpec((None, tk, LANES),
                               lambda l, i, j: (l, i, (src_off + (j + 1) * wb) // LANES))],
        out_specs=pl.BlockSpec((None, tk, wb), lambda l, i, j: (l, i, j)),
        out_shape=jax.ShapeDtypeStruct((depth, k, n_out), _BF16),
        compiler_params=_cparams("parallel", "parallel", "parallel"),
        name="shift_cast_weights",
    )(w, w)


def _norm_kernel(x_ref, w_ref, o_ref):
    x = x_ref[...]
    ms = jnp.mean(x * x, axis=-1, keepdims=True)
    o_ref[...] = (x * lax.rsqrt(ms + EPS) * w_ref[...]).astype(o_ref.dtype)


def _rmsnorm(x, w, out_dtype, tm):
    m, d = x.shape
    return pl.pallas_call(
        _norm_kernel,
        grid=(m // tm,),
        in_specs=[pl.BlockSpec((tm, d), lambda i: (i, 0)),
                  pl.BlockSpec((1, d), lambda i: (0, 0))],
        out_specs=pl.BlockSpec((tm, d), lambda i: (i, 0)),
        out_shape=jax.ShapeDtypeStruct((m, d), out_dtype),
        compiler_params=_cparams("parallel"),
        name="rmsnorm",
    )(x, w.reshape(1, d))


def _inproj_kernel(a_ref, w_ref, o_ref):
    o_ref[...] = jnp.dot(a_ref[...], w_ref[...], preferred_element_type=_F32)


def _inproj(a, w, layer, tm, tn):
    m, k = a.shape
    n = w.shape[2]
    return pl.pallas_call(
        _inproj_kernel,
        grid=(m // tm, n // tn),
        in_specs=[pl.BlockSpec((tm, k), lambda i, j: (i, 0)),
                  pl.BlockSpec((None, k, tn), lambda i, j: (layer, 0, j))],
        out_specs=pl.BlockSpec((tm, tn), lambda i, j: (i, j)),
        out_shape=jax.ShapeDtypeStruct((m, n), _F32),
        compiler_params=_cparams("parallel", "arbitrary"),
        name="inproj",
    )(a, w)


def _outproj_kernel(a1_ref, a2_ref, w1_ref, w2_ref, x_ref, o_ref):
    acc = jnp.dot(a1_ref[...], w1_ref[...], preferred_element_type=_F32)
    acc = acc + jnp.dot(a2_ref[...], w2_ref[...], preferred_element_type=_F32)
    o_ref[...] = x_ref[...] + acc


def _outproj(a1, a2, w, layer, x, tm, tn):
    m, k = a1.shape
    n = w.shape[2]
    return pl.pallas_call(
        _outproj_kernel,
        grid=(m // tm, n // tn),
        in_specs=[pl.BlockSpec((tm, k), lambda i, j: (i, 0)),
                  pl.BlockSpec((tm, k), lambda i, j: (i, 0)),
                  pl.BlockSpec((None, k, tn), lambda i, j: (layer, 0, j)),
                  pl.BlockSpec((None, k, tn), lambda i, j: (layer, 1, j)),
                  pl.BlockSpec((tm, tn), lambda i, j: (i, j))],
        out_specs=pl.BlockSpec((tm, tn), lambda i, j: (i, j)),
        out_shape=jax.ShapeDtypeStruct((m, n), _F32),
        compiler_params=_cparams("parallel", "arbitrary"),
        name="outproj",
    )(a1, a2, w, w, x)


def _conv_slabs(dm, in_refs, tail_ref, cw_ref, cb_ref, xc_ref):
    L = CHUNK
    col0 = 0
    for ref in in_refs:
        width = ref.shape[1]
        slab = _largest_divisor((512, 256, LANES), width)
        rowi = lax.broadcasted_iota(jnp.int32, (SUBLANES, slab), 0)
        for s in range(width // slab):
            src = slice(s * slab, (s + 1) * slab)
            dst = slice(col0 + s * slab, col0 + (s + 1) * slab)
            wts = [jnp.broadcast_to(cw_ref[j:j + 1, dst], (SUBLANES, slab)) for j in range(CONV_K)]
            bias = jnp.broadcast_to(cb_ref[:, dst], (SUBLANES, slab))
            prev = tail_ref[:, dst]
            prev_rolled = [pltpu.roll(prev, sh, axis=0) for sh in range(1, CONV_K)]
            for i in range(L // SUBLANES):
                cur = ref[i * SUBLANES:(i + 1) * SUBLANES, src]
                cur_rolled = [pltpu.roll(cur, sh, axis=0) for sh in range(1, CONV_K)]
                acc = bias
                for j in range(CONV_K - 1):
                    sh = CONV_K - 1 - j
                    tap = jnp.where(rowi >= sh, cur_rolled[sh - 1], prev_rolled[sh - 1])
                    acc = acc + wts[j] * tap
                acc = acc + wts[CONV_K - 1] * cur
                xc_ref[i * SUBLANES:(i + 1) * SUBLANES, dst] = _silu(acc)
                prev_rolled = cur_rolled
            tail_ref[:, dst] = ref[L - SUBLANES:L, src]
        col0 += width


def _ssd_prompt_kernel(dm, z_ref, xs_ref, b_ref, c_ref, dt_ref, cw_ref, cb_ref, dtb_ref, alog_ref,
                       dskip_ref, nw_ref, e_ref, y_ref, conv_ref, st_out_ref, tail_ref, xc_ref, st_ref):
    L, W, GN, GW, N = CHUNK, dm.W, dm.GN, dm.GW, dm.N
    c = pl.program_id(1)
    nc = pl.num_programs(1)

    @pl.when(c == 0)
    def _():
        tail_ref[...] = jnp.zeros_like(tail_ref)
        st_ref[...] = jnp.zeros_like(st_ref)

    _conv_slabs(dm, (xs_ref, b_ref, c_ref), tail_ref, cw_ref, cb_ref, xc_ref)
    keep = CONV_K - 1
    conv_ref[0, :, 0:W] = xs_ref[L - keep:L, :]
    conv_ref[0, :, W:W + GN] = b_ref[L - keep:L, :]
    conv_ref[0, :, W + GN:W + 2 * GN] = c_ref[L - keep:L, :]

    dt = _softplus(dt_ref[...] + dtb_ref[...])
    a_neg = -jnp.exp(alog_ref[...])
    la = dt * a_neg
    row = lax.broadcasted_iota(jnp.int32, (L, L), 0)
    col = lax.broadcasted_iota(jnp.int32, (L, L), 1)
    causal = row >= col
    upper = (row <= col).astype(_BF16)
    cum_t = None
    for part in _split3(la.T):
        d = jnp.dot(part, upper, preferred_element_type=_F32)
        cum_t = d if cum_t is None else cum_t + d
    cum = cum_t.T
    ecum = jnp.exp(cum)
    to_end = jnp.exp(cum[L - 1:L, :] - cum)
    stacked = jnp.concatenate([dt, ecum, to_end], axis=0)
    s_hi, s_lo = _split2(stacked)
    lane = lax.broadcasted_iota(jnp.int32, (L, LANES), 1)
    low_half = lane < dm.P

    for g in range(dm.G):
        sl = slice(g * GW, (g + 1) * GW)
        e_g = e_ref[:, sl]
        ex = (jnp.dot(s_hi, e_g, preferred_element_type=_F32)
              + jnp.dot(s_lo, e_g, preferred_element_type=_F32))
        dt_e, ecum_e, toe_e = ex[0:L], ex[L:2 * L], ex[2 * L:3 * L]
        xs_g = xc_ref[:, sl]
        xdt = xs_g * dt_e
        b_g = xc_ref[:, W + g * N:W + (g + 1) * N]
        c_g = xc_ref[:, W + GN + g * N:W + GN + (g + 1) * N]
        b_bf = b_g.astype(_BF16)
        c_bf = c_g.astype(_BF16)
        scores = lax.dot_general(c_bf, b_bf, _NT, preferred_element_type=_F32)
        st_g = st_ref[:, sl]
        y_parts = []
        for j in range(GW // LANES):
            xpair = xdt[:, j * LANES:(j + 1) * LANES].astype(_BF16)
            acc = None
            for which in range(2):
                h = g * dm.HPG + 2 * j + which
                seg = cum[:, h:h + 1] - cum_t[h:h + 1, :]
                decay = jnp.exp(jnp.where(causal, seg, -jnp.inf))
                m_h = (scores * decay).astype(_BF16)
                keep_lanes = low_half if which == 0 else jnp.logical_not(low_half)
                x_h = jnp.where(keep_lanes, xpair, jnp.zeros_like(xpair))
                d = jnp.dot(m_h, x_h, preferred_element_type=_F32)
                acc = d if acc is None else acc + d
            y_parts.append(acc)
        y = jnp.concatenate(y_parts, axis=1)
        y = y + jnp.dot(c_bf, st_g.astype(_BF16), preferred_element_type=_F32) * ecum_e
        w_g = (xdt * toe_e).astype(_BF16)
        cs = lax.dot_general(b_bf, w_g, _TN, preferred_element_type=_F32)
        st_ref[:, sl] = st_g * ecum_e[L - 1:L, :] + cs
        y = y + dskip_ref[:, sl] * xs_g
        y = y * _silu(z_ref[:, sl])
        ms = jnp.mean(y * y, axis=-1, keepdims=True)
        y_ref[:, sl] = (y * lax.rsqrt(ms + EPS) * nw_ref[:, sl]).astype(y_ref.dtype)

    @pl.when(c == nc - 1)
    def _():
        for j in range(W // LANES):
            st_out_ref[0, j * LANES:(j + 1) * LANES, :] = st_ref[:, j * LANES:(j + 1) * LANES].T


def _ssd_prompt(dm, layer, proj_a, nb, conv_w, conv_b, dtb, alog, dskip_e, nw, e_mat):
    L, W, GN = CHUNK, dm.W, dm.GN
    m = proj_a.shape[0]
    nc = m // nb // L
    rowblk = lambda b, c: b * nc + c
    par = lambda b, c: (layer, 0, 0)
    in_specs = [
        pl.BlockSpec((L, W), lambda b, c: (rowblk(b, c), 0)),
        pl.BlockSpec((L, W), lambda b, c: (rowblk(b, c), 1)),
        pl.BlockSpec((L, GN), lambda b, c: (rowblk(b, c), 2 * W // GN)),
        pl.BlockSpec((L, GN), lambda b, c: (rowblk(b, c), 2 * W // GN + 1)),
        pl.BlockSpec((L, LANES), lambda b, c: (rowblk(b, c), dm.o_dt // LANES)),
        pl.BlockSpec((None, CONV_K, dm.CD), par),
        pl.BlockSpec((None, 1, dm.CD), par),
        pl.BlockSpec((None, 1, LANES), par),
        pl.BlockSpec((None, 1, LANES), par),
        pl.BlockSpec((None, 1, W), par),
        pl.BlockSpec((None, 1, W), par),
        pl.BlockSpec((LANES, W), lambda b, c: (0, 0)),
    ]
    out_specs = [
        pl.BlockSpec((L, W), lambda b, c: (rowblk(b, c), 0)),
        pl.BlockSpec((1, CONV_K - 1, dm.CD), lambda b, c: (b, 0, 0)),
        pl.BlockSpec((1, W, dm.N), lambda b, c: (b, 0, 0)),
    ]
    out_shape = [
        jax.ShapeDtypeStruct((m, W), _BF16),
        jax.ShapeDtypeStruct((nb, CONV_K - 1, dm.CD), _F32),
        jax.ShapeDtypeStruct((nb, W, dm.N), _F32),
    ]
    return pl.pallas_call(
        functools.partial(_ssd_prompt_kernel, dm),
        grid=(nb, nc),
        in_specs=in_specs,
        out_specs=out_specs,
        out_shape=out_shape,
        scratch_shapes=[pltpu.VMEM((SUBLANES, dm.CD), _F32),
                        pltpu.VMEM((L, dm.CD), _F32),
                        pltpu.VMEM((dm.N, W), _F32)],
        compiler_params=_cparams("arbitrary", "arbitrary"),
        name="ssd_prompt",
    )(proj_a, proj_a, proj_a, proj_a, proj_a, conv_w, conv_b, dtb, alog, dskip_e, nw, e_mat)


def _ret_log_gammas(n):
    return [float(np.log1p(-np.exp2(np.float32(-5.0 - h)))) for h in range(n)]


def _rotate(x, cos2, sin2):
    return x * cos2 + pltpu.roll(x, LANES // 2, axis=1) * sin2


def _ret_prompt_kernel(dm, q_ref, k_ref, v_ref, g_ref, cos_ref, sin_ref, nw_ref,
                       y_ref, st_out_ref, dec_ref, ec_ref, te_ref, st_ref):
    L, DK, DV = CHUNK, dm.DK, dm.DV
    b = pl.program_id(0)
    c = pl.program_id(1)
    nc = pl.num_programs(1)
    lgs = _ret_log_gammas(dm.RH)

    @pl.when(jnp.logical_and(b == 0, c == 0))
    def _():
        row = lax.broadcasted_iota(jnp.int32, (L, L), 0)
        col = lax.broadcasted_iota(jnp.int32, (L, L), 1)
        diff = (row - col).astype(_F32)
        rowf = row.astype(_F32)
        for h in range(dm.RH):
            dec_ref[h] = jnp.exp(jnp.where(row >= col, diff * lgs[h], -jnp.inf))
            ec_ref[h] = jnp.exp((rowf + 1.0) * lgs[h])
            te_ref[h] = jnp.exp((float(L - 1) - rowf) * lgs[h])

    @pl.when(c == 0)
    def _():
        st_ref[...] = jnp.zeros_like(st_ref)

    cos2 = cos_ref[...]
    sin2 = sin_ref[...]
    scale = DK ** -0.5
    for h in range(dm.RH):
        qs = slice(h * DK, (h + 1) * DK)
        vs = slice(h * DV, (h + 1) * DV)
        qr = _rotate(q_ref[:, qs], cos2, sin2)
        kr = _rotate(k_ref[:, qs], cos2, sin2) * scale
        q_bf = qr.astype(_BF16)
        k_bf = kr.astype(_BF16)
        v_h = v_ref[:, vs]
        scores = lax.dot_general(q_bf, k_bf, _NT, preferred_element_type=_F32)
        m_h = (scores * dec_ref[h]).astype(_BF16)
        st_h = st_ref[h]
        ec = ec_ref[h]
        te = te_ref[h]
        y = jnp.dot(m_h, v_h.astype(_BF16), preferred_element_type=_F32)
        y_inter = jnp.dot(q_bf, st_h.astype(_BF16), preferred_element_type=_F32)
        y = y + y_inter * jnp.concatenate([ec, ec], axis=1)
        vt = (v_h * jnp.concatenate([te, te], axis=1)).astype(_BF16)
        cs = lax.dot_general(k_bf, vt, _TN, preferred_element_type=_F32)
        st_ref[h] = st_h * ec[L - 1:L, 0:1] + cs
        ms = jnp.mean(y * y, axis=-1, keepdims=True)
        y = y * lax.rsqrt(ms + EPS) * nw_ref[:, vs] * _silu(g_ref[:, vs])
        y_ref[:, vs] = y.astype(y_ref.dtype)

    @pl.when(c == nc - 1)
    def _():
        for h in range(dm.RH):
            st_out_ref[0, h * DV:(h + 1) * DV, :] = st_ref[h].T


def _ret_prompt(dm, layer, proj_b, nb, cos2, sin2, nw):
    L, W, QK = CHUNK, dm.W, dm.QK
    m = proj_b.shape[0]
    nc = m // nb // L
    rowblk = lambda b, c: b * nc + c
    in_specs = [
        pl.BlockSpec((L, QK), lambda b, c: (rowblk(b, c), 0)),
        pl.BlockSpec((L, QK), lambda b, c: (rowblk(b, c), 1)),
        pl.BlockSpec((L, W), lambda b, c: (rowblk(b, c), 1)),
        pl.BlockSpec((L, W), lambda b, c: (rowblk(b, c), 2)),
        pl.BlockSpec((L, LANES), lambda b, c: (c, 0)),
        pl.BlockSpec((L, LANES), lambda b, c: (c, 0)),
        pl.BlockSpec((None, 1, W), lambda b, c: (layer, 0, 0)),
    ]
    out_specs = [
        pl.BlockSpec((L, W), lambda b, c: (rowblk(b, c), 0)),
        pl.BlockSpec((1, dm.RH * dm.DV, dm.DK), lambda b, c: (b, 0, 0)),
    ]
    out_shape = [
        jax.ShapeDtypeStruct((m, W), _BF16),
        jax.ShapeDtypeStruct((nb, dm.RH * dm.DV, dm.DK), _F32),
    ]
    return pl.pallas_call(
        functools.partial(_ret_prompt_kernel, dm),
        grid=(nb, nc),
        in_specs=in_specs,
        out_specs=out_specs,
        out_shape=out_shape,
        scratch_shapes=[pltpu.VMEM((dm.RH, L, L), _F32),
                        pltpu.VMEM((dm.RH, L, LANES), _F32),
                        pltpu.VMEM((dm.RH, L, LANES), _F32),
                        pltpu.VMEM((dm.RH, dm.DK, dm.DV), _F32)],
        compiler_params=_cparams("arbitrary", "arbitrary"),
        name="ret_prompt",
    )(proj_b, proj_b, proj_b, proj_b, cos2, sin2, nw)


def _block_mask(rows, width, block):
    lane = lax.broadcasted_iota(jnp.int32, (rows, width), 1)
    lo = lax.broadcasted_iota(jnp.int32, (rows, width), 0) * block
    return jnp.logical_and(lane >= lo, lane < lo + block)


def _own_block(mask, x):
    return jnp.sum(jnp.where(mask, x, 0.0), axis=0, keepdims=True)


def _pad_rows(x, rows):
    if x.shape[0] == rows:
        return x
    return jnp.concatenate([x, jnp.zeros((rows - x.shape[0], x.shape[1]), x.dtype)], axis=0)


def _ssd_step_one(dm, i, z_ref, xs_ref, b_ref, c_ref, dt_ref, cst_ref, st_ref, cw_ref, cb_ref,
                  dtb_ref, alog_ref, dskip_ref, nw_ref, e_ref, y_ref, conv_ref, st_out_ref):
    W, GN, GW, N, G = dm.W, dm.GN, dm.GW, dm.N, dm.G
    xbc = jnp.concatenate([xs_ref[i], b_ref[i], c_ref[i]], axis=1)
    cst = cst_ref[i]
    acc = cb_ref[...]
    for j in range(CONV_K - 1):
        acc = acc + cw_ref[j:j + 1, :] * cst[j:j + 1, :]
    acc = acc + cw_ref[CONV_K - 1:CONV_K, :] * xbc
    xc = _silu(acc)
    conv_ref[i] = jnp.concatenate([cst[1:CONV_K - 1, :], xbc], axis=0)

    dt = _softplus(dt_ref[i] + dtb_ref[...])
    decay = jnp.exp(dt * (-jnp.exp(alog_ref[...])))
    r8 = lax.broadcasted_iota(jnp.int32, (SUBLANES, LANES), 0)
    stacked = jnp.where(r8 == 0, jnp.broadcast_to(dt, (SUBLANES, LANES)),
                        jnp.where(r8 == 1, jnp.broadcast_to(decay, (SUBLANES, LANES)), 0.0))
    s_hi = stacked.astype(_BF16).astype(_F32)
    s_hl = jnp.concatenate([s_hi, stacked - s_hi], axis=0).astype(_BF16)
    ex = jnp.dot(s_hl, e_ref[...], preferred_element_type=_F32)
    ex = ex[0:SUBLANES] + ex[SUBLANES:2 * SUBLANES]
    dt_e, dec_e = ex[0:1], ex[1:2]
    xs_c = xc[:, 0:W]
    xdt = xs_c * dt_e
    b8 = jnp.concatenate([xc[:, W + g * N:W + (g + 1) * N] for g in range(G)], axis=0)
    c8 = jnp.concatenate([xc[:, W + GN + g * N:W + GN + (g + 1) * N] for g in range(G)], axis=0)
    gmask = _block_mask(G, W, GW)

    s_old = st_ref[i]
    y_all = lax.dot_general(c8.astype(_BF16), s_old.astype(_BF16), _NT,
                            preferred_element_type=_F32)
    y_old = _own_block(gmask, y_all)
    cb_e = _own_block(gmask, jnp.sum(c8 * b8, axis=-1, keepdims=True))
    y = dec_e * y_old + cb_e * xdt

    d_hi = dec_e.astype(_BF16).astype(_F32)
    d_lo = dec_e - d_hi
    kp = -(-3 * G // 16) * 16
    lhs = jnp.concatenate([jnp.where(gmask, d_hi, 0.0), jnp.where(gmask, d_lo, 0.0),
                           jnp.where(gmask, xdt, 0.0)], axis=0)
    lhs = _pad_rows(lhs, kp).astype(_BF16)
    ones_zeros = jnp.concatenate([jnp.ones((2 * G, N), _F32), jnp.zeros((2 * G, N), _F32)], axis=1)
    rhs = jnp.concatenate([ones_zeros, jnp.concatenate([jnp.zeros((G, N), _F32), b8], axis=1)], axis=0)
    rhs = _pad_rows(rhs, kp).astype(_BF16)
    res = lax.dot_general(lhs, rhs, _TN, preferred_element_type=_F32)
    st_out_ref[i] = s_old * res[:, 0:N] + res[:, N:2 * N]

    y = y + dskip_ref[...] * xs_c
    y = y * _silu(z_ref[i])
    ms = _own_block(gmask, jnp.sum(jnp.where(gmask, y * y, 0.0), axis=-1, keepdims=True) * (1.0 / GW))
    y_ref[i] = (y * lax.rsqrt(ms + EPS) * nw_ref[...]).astype(y_ref.dtype)


def _ssd_step_kernel(dm, bb, *refs):
    refs = refs[:14] + refs[-3:]
    for i in range(bb):
        _ssd_step_one(dm, i, *refs)


def _ssd_step(dm, layer, bb, proj3, state_conv, state_ssd, prev_out, conv_w, conv_b, dtb, alog,
              dskip_e, nw, e_mat):
    W, GN = dm.W, dm.GN
    nb = proj3.shape[0]
    par = lambda b: (layer, 0, 0)
    in_specs = [
        pl.BlockSpec((bb, 1, W), lambda b: (b, 0, 0)),
        pl.BlockSpec((bb, 1, W), lambda b: (b, 0, 1)),
        pl.BlockSpec((bb, 1, GN), lambda b: (b, 0, 2 * W // GN)),
        pl.BlockSpec((bb, 1, GN), lambda b: (b, 0, 2 * W // GN + 1)),
        pl.BlockSpec((bb, 1, LANES), lambda b: (b, 0, dm.o_dt // LANES)),
        pl.BlockSpec((None, bb, CONV_K - 1, dm.CD), lambda b: (layer, b, 0, 0)),
        pl.BlockSpec((None, bb, W, dm.N), lambda b: (layer, b, 0, 0)),
        pl.BlockSpec((None, CONV_K, dm.CD), par),
        pl.BlockSpec((None, 1, dm.CD), par),
        pl.BlockSpec((None, 1, LANES), par),
        pl.BlockSpec((None, 1, LANES), par),
        pl.BlockSpec((None, 1, W), par),
        pl.BlockSpec((None, 1, W), par),
        pl.BlockSpec((LANES, W), lambda b: (0, 0)),
    ]
    args = [proj3, proj3, proj3, proj3, proj3, state_conv, state_ssd, conv_w, conv_b, dtb, alog,
            dskip_e, nw, e_mat]
    aliases = {}
    if prev_out is not None:
        in_specs.append(pl.BlockSpec(memory_space=pl.ANY))
        args.append(prev_out)
        aliases = {len(args) - 1: 2}
    out_specs = [
        pl.BlockSpec((bb, 1, W), lambda b: (b, 0, 0)),
        pl.BlockSpec((bb, CONV_K - 1, dm.CD), lambda b: (b, 0, 0)),
        pl.BlockSpec((None, bb, W, dm.N), lambda b: (layer, b, 0, 0)),
    ]
    out_shape = [
        jax.ShapeDtypeStruct((nb, 1, W), _BF16),
        jax.ShapeDtypeStruct((nb, CONV_K - 1, dm.CD), _F32),
        jax.ShapeDtypeStruct(state_ssd.shape, _F32),
    ]
    return pl.pallas_call(
        functools.partial(_ssd_step_kernel, dm, bb),
        grid=(nb // bb,),
        in_specs=in_specs,
        out_specs=out_specs,
        out_shape=out_shape,
        input_output_aliases=aliases,
        compiler_params=_cparams("arbitrary"),
        name="ssd_step",
    )(*args)


def _ret_step_one(dm, i, q_ref, k_ref, v_ref, g_ref, cos_ref, sin_ref, gam_ref, st_ref, nw_ref,
                  y_ref, st_out_ref):
    DK, DV, RH, W, QK = dm.DK, dm.DV, dm.RH, dm.W, dm.QK
    gammas = [float(np.exp(np.float32(lg))) for lg in _ret_log_gammas(RH)]
    cos_t, sin_t = cos_ref[...], sin_ref[...]
    lane = lax.broadcasted_iota(jnp.int32, (1, QK), 1)
    first_half = jnp.bitwise_and(lane, DK - 1) < DK // 2

    def rotate_heads(x):
        partner = jnp.where(first_half, pltpu.roll(x, QK - DK // 2, axis=1), pltpu.roll(x, DK // 2, axis=1))
        return x * cos_t + partner * sin_t

    qr = rotate_heads(q_ref[i])
    kr = rotate_heads(k_ref[i]) * (DK ** -0.5)
    q16 = jnp.concatenate([qr[:, h * DK:(h + 1) * DK] for h in range(RH)], axis=0)
    k16 = jnp.concatenate([kr[:, h * DK:(h + 1) * DK] for h in range(RH)], axis=0)
    v = v_ref[i]
    hmask = _block_mask(RH, W, DV)
    kp = -(-RH // 16) * 16

    s_old = st_ref[i]
    y_all = lax.dot_general(_pad_rows(q16, kp).astype(_BF16), s_old.astype(_BF16), _NT,
                            preferred_element_type=_F32)[0:RH]
    y_old = _own_block(hmask, y_all)
    qk_e = _own_block(hmask, jnp.sum(q16 * k16, axis=-1, keepdims=True))
    y = gam_ref[...] * y_old + qk_e * v
    lhs = _pad_rows(jnp.where(hmask, v, 0.0), kp).astype(_BF16)
    outer = lax.dot_general(lhs, _pad_rows(k16, kp).astype(_BF16), _TN,
                            preferred_element_type=_F32)
    for h in range(RH):
        rows = slice(h * DV, (h + 1) * DV)
        st_out_ref[i, rows, :] = gammas[h] * st_ref[i, rows, :] + outer[rows, :]
    ms = _own_block(hmask, jnp.sum(jnp.where(hmask, y * y, 0.0), axis=-1, keepdims=True) * (1.0 / DV))
    y = y * lax.rsqrt(ms + EPS) * nw_ref[...] * _silu(g_ref[i])
    y_ref[i] = y.astype(y_ref.dtype)


def _ret_step_kernel(dm, bb, *refs):
    refs = refs[:9] + refs[-2:]
    for i in range(bb):
        _ret_step_one(dm, i, *refs)


def _ret_step(dm, layer, bb, proj3, state_ret, prev_out, cos_t, sin_t, gam_e, nw):
    W, QK = dm.W, dm.QK
    nb = proj3.shape[0]
    cst = lambda b: (0, 0)
    in_specs = [
        pl.BlockSpec((bb, 1, QK), lambda b: (b, 0, 0)),
        pl.BlockSpec((bb, 1, QK), lambda b: (b, 0, 1)),
        pl.BlockSpec((bb, 1, W), lambda b: (b, 0, 1)),
        pl.BlockSpec((bb, 1, W), lambda b: (b, 0, 2)),
        pl.BlockSpec((1, QK), cst),
        pl.BlockSpec((1, QK), cst),
        pl.BlockSpec((1, W), cst),
        pl.BlockSpec((None, bb, dm.RH * dm.DV, dm.DK), lambda b: (layer, b, 0, 0)),
        pl.BlockSpec((None, 1, W), lambda b: (layer, 0, 0)),
    ]
    args = [proj3, proj3, proj3, proj3, cos_t, sin_t, gam_e, state_ret, nw]
    aliases = {}
    if prev_out is not None:
        in_specs.append(pl.BlockSpec(memory_space=pl.ANY))
        args.append(prev_out)
        aliases = {len(args) - 1: 1}
    out_specs = [
        pl.BlockSpec((bb, 1, W), lambda b: (b, 0, 0)),
        pl.BlockSpec((None, bb, dm.RH * dm.DV, dm.DK), lambda b: (layer, b, 0, 0)),
    ]
    out_shape = [
        jax.ShapeDtypeStruct((nb, 1, W), _BF16),
        jax.ShapeDtypeStruct(state_ret.shape, _F32),
    ]
    return pl.pallas_call(
        functools.partial(_ret_step_kernel, dm, bb),
        grid=(nb // bb,),
        in_specs=in_specs,
        out_specs=out_specs,
        out_shape=out_shape,
        input_output_aliases=aliases,
        compiler_params=_cparams("arbitrary"),
        name="ret_step",
    )(*args)


def _rope_tables(dk, positions):
    half = dk // 2
    inv = 1.0 / (ROPE_BASE ** (jnp.arange(half, dtype=_F32) / half))
    ang = positions.astype(_F32)[:, None] * inv[None, :]
    cos = jnp.cos(ang)
    sin = jnp.sin(ang)
    return jnp.concatenate([cos, cos], axis=1), jnp.concatenate([-sin, sin], axis=1)


def _pad_lanes(v):
    return jnp.pad(v, ((0, 0), (0, LANES - v.shape[1])))[:, None, :]


def kernel(x_prompt, x_sample, state_conv, state_ssd, state_ret, norm_w, w_in, conv_w, conv_b, dt_bias,
           a_log, d_skip, ssd_norm_w, ret_norm_w, w_out, final_norm_w):
    bp, tp, d_model = x_prompt.shape
    bs, ts, _ = x_sample.shape
    depth = norm_w.shape[0]
    assert ts == 1 and tp % CHUNK == 0
    dm = Dims(d_model, dt_bias.shape[1], conv_w.shape[2], state_ret.shape[2], state_ret.shape[3],
              state_ret.shape[4])
    assert w_in.shape[2] == dm.IN_COLS
    W = dm.W

    tk = min(1024, d_model)
    w_a = _cast_weights(w_in, dm.NA, dm.o_dt + dm.H, tk, dm.TN)
    wb = _largest_divisor((2048, 1024, 512), dm.o_dt, dm.NB)
    w_b = _shift_cast_weights(w_in, dm.o_dt, dm.H, dm.NB, min(512, d_model), wb)
    w_o = _cast_weights(w_out, d_model, None, tk, dm.TN)

    heads = jnp.arange(LANES)[:, None]
    lanes = jnp.arange(W)[None, :]
    e_mat = (lanes // dm.P == heads).astype(_BF16)
    cos_p, sin_p = _rope_tables(dm.DK, jnp.arange(tp))
    cos_s, sin_s = _rope_tables(dm.DK, PAST_LEN + jnp.arange(ts))
    cos_s, sin_s = jnp.tile(cos_s, (1, dm.RH)), jnp.tile(sin_s, (1, dm.RH))
    gam_e = jnp.repeat(jnp.exp(jnp.asarray(_ret_log_gammas(dm.RH), _F32)), dm.DV)[None, :]
    cb3 = conv_b[:, None, :]
    dtb3, alog3 = _pad_lanes(dt_bias), _pad_lanes(a_log)
    dskip3 = jnp.repeat(d_skip, dm.P, axis=1)[:, None, :]
    snw3, rnw3 = ssd_norm_w[:, None, :], ret_norm_w[:, None, :]

    st_ssd_in = state_ssd.reshape(depth, bs, dm.H * dm.P, dm.N)
    st_ret_in = state_ret.reshape(depth, bs, dm.RH * dm.DV, dm.DK)

    xp = x_prompt.reshape(bp * tp, d_model)
    xs = x_sample.reshape(bs * ts, d_model)
    mp = xp.shape[0]
    tm_p = min(1024, mp)
    tm_o = min(512, mp)
    tm_n = min(256, mp)
    bb = 2 if bs % 2 == 0 else 1

    conv_p, ssd_p, ret_p, conv_s = [], [], [], []
    ssd_s = None
    ret_s = None
    for l in range(depth):
        h = _rmsnorm(xp, norm_w[l], _BF16, tm_n)
        proj_a = _inproj(h, w_a, l, tm_p, dm.TN)
        proj_b = _inproj(h, w_b, l, tm_p, dm.TN)
        y_ssd, c_new, s_new = _ssd_prompt(dm, l, proj_a, bp, conv_w, cb3, dtb3, alog3, dskip3, snw3, e_mat)
        y_ret, r_new = _ret_prompt(dm, l, proj_b, bp, cos_p, sin_p, rnw3)
        xp = _outproj(y_ssd, y_ret, w_o, l, xp, tm_o, dm.TN)
        conv_p.append(c_new)
        ssd_p.append(s_new.reshape(bp, dm.H, dm.P, dm.N))
        ret_p.append(r_new.reshape(bp, dm.RH, dm.DV, dm.DK))

        h = _rmsnorm(xs, norm_w[l], _BF16, bs)
        proj3_a = _inproj(h, w_a, l, bs, dm.TN).reshape(bs, 1, dm.NA)
        proj3_b = _inproj(h, w_b, l, bs, dm.TN).reshape(bs, 1, dm.NB)
        y_ssd, c_new, ssd_s = _ssd_step(dm, l, bb, proj3_a, state_conv, st_ssd_in, ssd_s, conv_w, cb3,
                                        dtb3, alog3, dskip3, snw3, e_mat)
        y_ret, ret_s = _ret_step(dm, l, bb, proj3_b, st_ret_in, ret_s, cos_s, sin_s, gam_e, rnw3)
        xs = _outproj(y_ssd.reshape(bs, W), y_ret.reshape(bs, W), w_o, l, xs, bs, dm.TN)
        conv_s.append(c_new)

    y_prompt = _rmsnorm(xp, final_norm_w, _F32, tm_n).reshape(bp, tp, d_model)
    y_sample = _rmsnorm(xs, final_norm_w, _F32, bs).reshape(bs, ts, d_model)
    return (y_prompt, y_sample, jnp.stack(conv_p), jnp.stack(ssd_p), jnp.stack(ret_p),
            jnp.stack(conv_s),
            ssd_s.reshape(depth, bs, dm.H, dm.P, dm.N),
            ret_s.reshape(depth, bs, dm.RH, dm.DV, dm.DK))
```

```python
import functools

import numpy as np
import jax
import jax.numpy as jnp
from jax import lax
from jax.experimental import pallas as pl
from jax.experimental.pallas import tpu as pltpu

SSD_HEADDIM = 64
SSD_STATE = 128
CONV_K = 4
CHUNK = 128
ROPE_BASE = 10000.0
EPS = 1e-5
PAST_LEN = 16384

LANES = 128
SUBLANES = 8
VMEM_LIMIT_BYTES = 56 * 1024 * 1024

_F32 = jnp.float32
_BF16 = jnp.bfloat16
_NT = (((1,), (1,)), ((), ()))
_TN = (((0,), (0,)), ((), ()))


def _cparams(*sem):
    return pltpu.CompilerParams(dimension_semantics=sem, vmem_limit_bytes=VMEM_LIMIT_BYTES)


def _silu(x):
    h = 0.5 * x
    return h + h * jnp.tanh(h)


def _softplus(x):
    return jnp.maximum(x, 0.0) + jnp.log1p(jnp.exp(-jnp.abs(x)))


def _split2(x):
    hi = x.astype(_BF16)
    lo = (x - hi.astype(_F32)).astype(_BF16)
    return hi, lo


def _split3(x):
    hi = x.astype(_BF16)
    r = x - hi.astype(_F32)
    mid = r.astype(_BF16)
    lo = (r - mid.astype(_F32)).astype(_BF16)
    return hi, mid, lo


def _largest_divisor(candidates, *values):
    for c in candidates:
        if all(v % c == 0 for v in values):
            return c
    raise ValueError((candidates, values))


class Dims:
    def __init__(self, d_model, n_heads, conv_dim, ret_heads, ret_dv, ret_dk):
        self.D = d_model
        self.W = d_model
        self.H = n_heads
        self.P = SSD_HEADDIM
        self.N = SSD_STATE
        self.GN = (conv_dim - self.W) // 2
        self.G = self.GN // self.N
        self.HPG = self.H // self.G
        self.GW = self.HPG * self.P
        self.CD = conv_dim
        self.RH = ret_heads
        self.DV = ret_dv
        self.DK = ret_dk
        self.QK = ret_heads * ret_dk
        self.TN = 512
        self.o_dt = 2 * self.W + 2 * self.GN
        self.NA = -(-(self.o_dt + LANES) // self.TN) * self.TN
        self.NB = 2 * self.QK + 2 * self.W
        self.IN_COLS = self.o_dt + self.H + self.NB
        assert self.H < LANES and self.P * 2 == LANES and self.N == LANES
        assert self.DK == LANES and self.DV == 2 * LANES and 2 * self.QK == self.W
        assert self.GW % LANES == 0 and self.W % self.GW == 0
        assert (2 * self.W) % self.GN == 0 and self.NB % self.TN == 0


def _cast_kernel(limit, w_ref, o_ref):
    x = w_ref[...]
    if limit is not None:
        col = pl.program_id(2) * x.shape[1] + lax.broadcasted_iota(jnp.int32, x.shape, 1)
        x = jnp.where(col < limit, x, 0.0)
    o_ref[...] = x.astype(o_ref.dtype)


def _cast_weights(w, n_out, limit, tk, tn):
    depth, k, _ = w.shape
    return pl.pallas_call(
        functools.partial(_cast_kernel, limit),
        grid=(depth, k // tk, n_out // tn),
        in_specs=[pl.BlockSpec((None, tk, tn), lambda l, i, j: (l, i, j))],
        out_specs=pl.BlockSpec((None, tk, tn), lambda l, i, j: (l, i, j)),
        out_shape=jax.ShapeDtypeStruct((depth, k, n_out), _BF16),
        compiler_params=_cparams("parallel", "parallel", "parallel"),
        name="cast_weights",
    )(w)


def _transpose_cast_kernel(row_off, limit, wt_ref, o_ref):
    x = wt_ref[0]
    if limit is not None:
        row = (row_off + pl.program_id(2) * x.shape[0]
               + lax.broadcasted_iota(jnp.int32, x.shape, 0))
        x = jnp.where(row < limit, x, 0.0)
    o_ref[...] = x.T.astype(o_ref.dtype)


def _transpose_cast_weights(wt, row_off, n_out, limit, tk, tn):
    depth, _, k = wt.shape
    return pl.pallas_call(
        functools.partial(_transpose_cast_kernel, row_off, limit),
        grid=(depth, k // tk, n_out // tn),
        in_specs=[pl.BlockSpec((pl.Element(1), pl.Element(tn), pl.Element(tk)),
                               lambda l, i, j: (l, pl.multiple_of(row_off + j * tn, SUBLANES),
                                                pl.multiple_of(i * tk, LANES)))],
        out_specs=pl.BlockSpec((None, tk, tn), lambda l, i, j: (l, i, j)),
        out_shape=jax.ShapeDtypeStruct((depth, k, n_out), _BF16),
        compiler_params=_cparams("parallel", "parallel", "parallel"),
        name="transpose_cast_weights",
    )(wt)


def _norm_kernel(x_ref, w_ref, o_ref):
    x = x_ref[...]
    ms = jnp.mean(x * x, axis=-1, keepdims=True)
    o_ref[...] = (x * lax.rsqrt(ms + EPS) * w_ref[...]).astype(o_ref.dtype)


def _rmsnorm(x, w, out_dtype, tm):
    m, d = x.shape
    return pl.pallas_call(
        _norm_kernel,
        grid=(m // tm,),
        in_specs=[pl.BlockSpec((tm, d), lambda i: (i, 0)),
                  pl.BlockSpec((1, d), lambda i: (0, 0))],
        out_specs=pl.BlockSpec((tm, d), lambda i: (i, 0)),
        out_shape=jax.ShapeDtypeStruct((m, d), out_dtype),
        compiler_params=_cparams("parallel"),
        name="rmsnorm",
    )(x, w.reshape(1, d))


def _inproj_kernel(a_ref, w_ref, o_ref):
    o_ref[...] = jnp.dot(a_ref[...], w_ref[...], preferred_element_type=_F32)


def _inproj(a, w, layer, tm, tn):
    m, k = a.shape
    n = w.shape[2]
    return pl.pallas_call(
        _inproj_kernel,
        grid=(m // tm, n // tn),
        in_specs=[pl.BlockSpec((tm, k), lambda i, j: (i, 0)),
                  pl.BlockSpec((None, k, tn), lambda i, j: (layer, 0, j))],
        out_specs=pl.BlockSpec((tm, tn), lambda i, j: (i, j)),
        out_shape=jax.ShapeDtypeStruct((m, n), _F32),
        compiler_params=_cparams("parallel", "arbitrary"),
        name="inproj",
    )(a, w)


def _outproj_kernel(a1_ref, a2_ref, w1_ref, w2_ref, x_ref, o_ref):
    acc = jnp.dot(a1_ref[...], w1_ref[...], preferred_element_type=_F32)
    acc = acc + jnp.dot(a2_ref[...], w2_ref[...], preferred_element_type=_F32)
    o_ref[...] = x_ref[...] + acc


def _outproj(a1, a2, w, layer, x, tm, tn):
    m, k = a1.shape
    n = w.shape[2]
    return pl.pallas_call(
        _outproj_kernel,
        grid=(m // tm, n // tn),
        in_specs=[pl.BlockSpec((tm, k), lambda i, j: (i, 0)),
                  pl.BlockSpec((tm, k), lambda i, j: (i, 0)),
                  pl.BlockSpec((None, k, tn), lambda i, j: (layer, 0, j)),
                  pl.BlockSpec((None, k, tn), lambda i, j: (layer, 1, j)),
                  pl.BlockSpec((tm, tn), lambda i, j: (i, j))],
        out_specs=pl.BlockSpec((tm, tn), lambda i, j: (i, j)),
        out_shape=jax.ShapeDtypeStruct((m, n), _F32),
        compiler_params=_cparams("parallel", "arbitrary"),
        name="outproj",
    )(a1, a2, w, w, x)


def _conv_slabs(dm, in_refs, tail_ref, cw_ref, cb_ref, xc_ref):
    L = CHUNK
    col0 = 0
    for ref in in_refs:
        width = ref.shape[1]
        slab = _largest_divisor((512, 256, LANES), width)
        rowi = lax.broadcasted_iota(jnp.int32, (SUBLANES, slab), 0)
        for s in range(width // slab):
            src = slice(s * slab, (s + 1) * slab)
            dst = slice(col0 + s * slab, col0 + (s + 1) * slab)
            wts = [jnp.broadcast_to(cw_ref[j:j + 1, dst], (SUBLANES, slab)) for j in range(CONV_K)]
            bias = jnp.broadcast_to(cb_ref[:, dst], (SUBLANES, slab))
            prev = tail_ref[:, dst]
            prev_rolled = [pltpu.roll(prev, sh, axis=0) for sh in range(1, CONV_K)]
            for i in range(L // SUBLANES):
                cur = ref[i * SUBLANES:(i + 1) * SUBLANES, src]
                cur_rolled = [pltpu.roll(cur, sh, axis=0) for sh in range(1, CONV_K)]
                acc = bias
                for j in range(CONV_K - 1):
                    sh = CONV_K - 1 - j
                    tap = jnp.where(rowi >= sh, cur_rolled[sh - 1], prev_rolled[sh - 1])
                    acc = acc + wts[j] * tap
                acc = acc + wts[CONV_K - 1] * cur
                xc_ref[i * SUBLANES:(i + 1) * SUBLANES, dst] = _silu(acc)
                prev_rolled = cur_rolled
            tail_ref[:, dst] = ref[L - SUBLANES:L, src]
        col0 += width


def _ssd_prompt_kernel(dm, z_ref, xs_ref, b_ref, c_ref, dt_ref, cw_ref, cb_ref, dtb_ref, alog_ref,
                       dskip_ref, nw_ref, e_ref, y_ref, conv_ref, st_out_ref, tail_ref, xc_ref, st_ref):
    L, W, GN, GW, N = CHUNK, dm.W, dm.GN, dm.GW, dm.N
    c = pl.program_id(1)
    nc = pl.num_programs(1)

    @pl.when(c == 0)
    def _():
        tail_ref[...] = jnp.zeros_like(tail_ref)
        st_ref[...] = jnp.zeros_like(st_ref)

    _conv_slabs(dm, (xs_ref, b_ref, c_ref), tail_ref, cw_ref, cb_ref, xc_ref)
    keep = CONV_K - 1
    conv_ref[0, :, 0:W] = xs_ref[L - keep:L, :]
    conv_ref[0, :, W:W + GN] = b_ref[L - keep:L, :]
    conv_ref[0, :, W + GN:W + 2 * GN] = c_ref[L - keep:L, :]

    dt = _softplus(dt_ref[...] + dtb_ref[...])
    a_neg = -jnp.exp(alog_ref[...])
    la = dt * a_neg
    row = lax.broadcasted_iota(jnp.int32, (L, L), 0)
    col = lax.broadcasted_iota(jnp.int32, (L, L), 1)
    causal = row >= col
    upper = (row <= col).astype(_BF16)
    cum_t = None
    for part in _split3(la.T):
        d = jnp.dot(part, upper, preferred_element_type=_F32)
        cum_t = d if cum_t is None else cum_t + d
    cum = cum_t.T
    ecum = jnp.exp(cum)
    to_end = jnp.exp(cum[L - 1:L, :] - cum)
    stacked = jnp.concatenate([dt, ecum, to_end], axis=0)
    s_hi, s_lo = _split2(stacked)
    lane = lax.broadcasted_iota(jnp.int32, (L, LANES), 1)
    low_half = lane < dm.P

    for g in range(dm.G):
        sl = slice(g * GW, (g + 1) * GW)
        e_g = e_ref[:, sl]
        ex = (jnp.dot(s_hi, e_g, preferred_element_type=_F32)
              + jnp.dot(s_lo, e_g, preferred_element_type=_F32))
        dt_e, ecum_e, toe_e = ex[0:L], ex[L:2 * L], ex[2 * L:3 * L]
        xs_g = xc_ref[:, sl]
        xdt = xs_g * dt_e
        b_g = xc_ref[:, W + g * N:W + (g + 1) * N]
        c_g = xc_ref[:, W + GN + g * N:W + GN + (g + 1) * N]
        b_bf = b_g.astype(_BF16)
        c_bf = c_g.astype(_BF16)
        scores = lax.dot_general(c_bf, b_bf, _NT, preferred_element_type=_F32)
        st_g = st_ref[:, sl]
        y_parts = []
        for j in range(GW // LANES):
            xpair = xdt[:, j * LANES:(j + 1) * LANES].astype(_BF16)
            acc = None
            for which in range(2):
                h = g * dm.HPG + 2 * j + which
                seg = cum[:, h:h + 1] - cum_t[h:h + 1, :]
                decay = jnp.exp(jnp.where(causal, seg, -jnp.inf))
                m_h = (scores * decay).astype(_BF16)
                keep_lanes = low_half if which == 0 else jnp.logical_not(low_half)
                x_h = jnp.where(keep_lanes, xpair, jnp.zeros_like(xpair))
                d = jnp.dot(m_h, x_h, preferred_element_type=_F32)
                acc = d if acc is None else acc + d
            y_parts.append(acc)
        y = jnp.concatenate(y_parts, axis=1)
        y = y + jnp.dot(c_bf, st_g.astype(_BF16), preferred_element_type=_F32) * ecum_e
        w_g = (xdt * toe_e).astype(_BF16)
        cs = lax.dot_general(b_bf, w_g, _TN, preferred_element_type=_F32)
        st_ref[:, sl] = st_g * ecum_e[L - 1:L, :] + cs
        y = y + dskip_ref[:, sl] * xs_g
        y = y * _silu(z_ref[:, sl])
        ms = jnp.mean(y * y, axis=-1, keepdims=True)
        y_ref[:, sl] = (y * lax.rsqrt(ms + EPS) * nw_ref[:, sl]).astype(y_ref.dtype)

    @pl.when(c == nc - 1)
    def _():
        for j in range(W // LANES):
            st_out_ref[0, j * LANES:(j + 1) * LANES, :] = st_ref[:, j * LANES:(j + 1) * LANES].T


def _ssd_prompt(dm, layer, proj_a, nb, conv_w, conv_b, dtb, alog, dskip_e, nw, e_mat):
    L, W, GN = CHUNK, dm.W, dm.GN
    m = proj_a.shape[0]
    nc = m // nb // L
    rowblk = lambda b, c: b * nc + c
    par = lambda b, c: (layer, 0, 0)
    in_specs = [
        pl.BlockSpec((L, W), lambda b, c: (rowblk(b, c), 0)),
        pl.BlockSpec((L, W), lambda b, c: (rowblk(b, c), 1)),
        pl.BlockSpec((L, GN), lambda b, c: (rowblk(b, c), 2 * W // GN)),
        pl.BlockSpec((L, GN), lambda b, c: (rowblk(b, c), 2 * W // GN + 1)),
        pl.BlockSpec((L, LANES), lambda b, c: (rowblk(b, c), dm.o_dt // LANES)),
        pl.BlockSpec((None, CONV_K, dm.CD), par),
        pl.BlockSpec((None, 1, dm.CD), par),
        pl.BlockSpec((None, 1, LANES), par),
        pl.BlockSpec((None, 1, LANES), par),
        pl.BlockSpec((None, 1, W), par),
        pl.BlockSpec((None, 1, W), par),
        pl.BlockSpec((LANES, W), lambda b, c: (0, 0)),
    ]
    out_specs = [
        pl.BlockSpec((L, W), lambda b, c: (rowblk(b, c), 0)),
        pl.BlockSpec((1, CONV_K - 1, dm.CD), lambda b, c: (b, 0, 0)),
        pl.BlockSpec((1, W, dm.N), lambda b, c: (b, 0, 0)),
    ]
    out_shape = [
        jax.ShapeDtypeStruct((m, W), _BF16),
        jax.ShapeDtypeStruct((nb, CONV_K - 1, dm.CD), _F32),
        jax.ShapeDtypeStruct((nb, W, dm.N), _F32),
    ]
    return pl.pallas_call(
        functools.partial(_ssd_prompt_kernel, dm),
        grid=(nb, nc),
        in_specs=in_specs,
        out_specs=out_specs,
        out_shape=out_shape,
        scratch_shapes=[pltpu.VMEM((SUBLANES, dm.CD), _F32),
                        pltpu.VMEM((L, dm.CD), _F32),
                        pltpu.VMEM((dm.N, W), _F32)],
        compiler_params=_cparams("arbitrary", "arbitrary"),
        name="ssd_prompt",
    )(proj_a, proj_a, proj_a, proj_a, proj_a, conv_w, conv_b, dtb, alog, dskip_e, nw, e_mat)


def _ret_log_gammas(n):
    return [float(np.log1p(-np.exp2(np.float32(-5.0 - h)))) for h in range(n)]


def _rotate(x, cos2, sin2):
    return x * cos2 + pltpu.roll(x, LANES // 2, axis=1) * sin2


def _ret_prompt_kernel(dm, q_ref, k_ref, v_ref, g_ref, cos_ref, sin_ref, nw_ref,
                       y_ref, st_out_ref, dec_ref, ec_ref, te_ref, st_ref):
    L, DK, DV = CHUNK, dm.DK, dm.DV
    b = pl.program_id(0)
    c = pl.program_id(1)
    nc = pl.num_programs(1)
    lgs = _ret_log_gammas(dm.RH)

    @pl.when(jnp.logical_and(b == 0, c == 0))
    def _():
        row = lax.broadcasted_iota(jnp.int32, (L, L), 0)
        col = lax.broadcasted_iota(jnp.int32, (L, L), 1)
        diff = (row - col).astype(_F32)
        rowf = row.astype(_F32)
        for h in range(dm.RH):
            dec_ref[h] = jnp.exp(jnp.where(row >= col, diff * lgs[h], -jnp.inf))
            ec_ref[h] = jnp.exp((rowf + 1.0) * lgs[h])
            te_ref[h] = jnp.exp((float(L - 1) - rowf) * lgs[h])

    @pl.when(c == 0)
    def _():
        st_ref[...] = jnp.zeros_like(st_ref)

    cos2 = cos_ref[...]
    sin2 = sin_ref[...]
    scale = DK ** -0.5
    for h in range(dm.RH):
        qs = slice(h * DK, (h + 1) * DK)
        vs = slice(h * DV, (h + 1) * DV)
        qr = _rotate(q_ref[:, qs], cos2, sin2)
        kr = _rotate(k_ref[:, qs], cos2, sin2) * scale
        q_bf = qr.astype(_BF16)
        k_bf = kr.astype(_BF16)
        v_h = v_ref[:, vs]
        scores = lax.dot_general(q_bf, k_bf, _NT, preferred_element_type=_F32)
        m_h = (scores * dec_ref[h]).astype(_BF16)
        st_h = st_ref[h]
        ec = ec_ref[h]
        te = te_ref[h]
        y = jnp.dot(m_h, v_h.astype(_BF16), preferred_element_type=_F32)
        y_inter = jnp.dot(q_bf, st_h.astype(_BF16), preferred_element_type=_F32)
        y = y + y_inter * jnp.concatenate([ec, ec], axis=1)
        vt = (v_h * jnp.concatenate([te, te], axis=1)).astype(_BF16)
        cs = lax.dot_general(k_bf, vt, _TN, preferred_element_type=_F32)
        st_ref[h] = st_h * ec[L - 1:L, 0:1] + cs
        ms = jnp.mean(y * y, axis=-1, keepdims=True)
        y = y * lax.rsqrt(ms + EPS) * nw_ref[:, vs] * _silu(g_ref[:, vs])
        y_ref[:, vs] = y.astype(y_ref.dtype)

    @pl.when(c == nc - 1)
    def _():
        for h in range(dm.RH):
            st_out_ref[0, h * DV:(h + 1) * DV, :] = st_ref[h].T


def _ret_prompt(dm, layer, proj_b, nb, cos2, sin2, nw):
    L, W, QK = CHUNK, dm.W, dm.QK
    m = proj_b.shape[0]
    nc = m // nb // L
    rowblk = lambda b, c: b * nc + c
    in_specs = [
        pl.BlockSpec((L, QK), lambda b, c: (rowblk(b, c), 0)),
        pl.BlockSpec((L, QK), lambda b, c: (rowblk(b, c), 1)),
        pl.BlockSpec((L, W), lambda b, c: (rowblk(b, c), 1)),
        pl.BlockSpec((L, W), lambda b, c: (rowblk(b, c), 2)),
        pl.BlockSpec((L, LANES), lambda b, c: (c, 0)),
        pl.BlockSpec((L, LANES), lambda b, c: (c, 0)),
        pl.BlockSpec((None, 1, W), lambda b, c: (layer, 0, 0)),
    ]
    out_specs = [
        pl.BlockSpec((L, W), lambda b, c: (rowblk(b, c), 0)),
        pl.BlockSpec((1, dm.RH * dm.DV, dm.DK), lambda b, c: (b, 0, 0)),
    ]
    out_shape = [
        jax.ShapeDtypeStruct((m, W), _BF16),
        jax.ShapeDtypeStruct((nb, dm.RH * dm.DV, dm.DK), _F32),
    ]
    return pl.pallas_call(
        functools.partial(_ret_prompt_kernel, dm),
        grid=(nb, nc),
        in_specs=in_specs,
        out_specs=out_specs,
        out_shape=out_shape,
        scratch_shapes=[pltpu.VMEM((dm.RH, L, L), _F32),
                        pltpu.VMEM((dm.RH, L, LANES), _F32),
                        pltpu.VMEM((dm.RH, L, LANES), _F32),
                        pltpu.VMEM((dm.RH, dm.DK, dm.DV), _F32)],
        compiler_params=_cparams("arbitrary", "arbitrary"),
        name="ret_prompt",
    )(proj_b, proj_b, proj_b, proj_b, cos2, sin2, nw)


def _block_mask(rows, width, block):
    lane = lax.broadcasted_iota(jnp.int32, (rows, width), 1)
    lo = lax.broadcasted_iota(jnp.int32, (rows, width), 0) * block
    return jnp.logical_and(lane >= lo, lane < lo + block)


def _own_block(mask, x):
    return jnp.sum(jnp.where(mask, x, 0.0), axis=0, keepdims=True)


def _pad_rows(x, rows):
    if x.shape[0] == rows:
        return x
    return jnp.concatenate([x, jnp.zeros((rows - x.shape[0], x.shape[1]), x.dtype)], axis=0)


def _ssd_step_one(dm, i, z_ref, xs_ref, b_ref, c_ref, dt_ref, cst_ref, st_ref, cw_ref, cb_ref,
                  dtb_ref, alog_ref, dskip_ref, nw_ref, e_ref, y_ref, conv_ref, st_out_ref):
    W, GN, GW, N, G = dm.W, dm.GN, dm.GW, dm.N, dm.G
    xbc = jnp.concatenate([xs_ref[i], b_ref[i], c_ref[i]], axis=1)
    cst = cst_ref[i]
    acc = cb_ref[...]
    for j in range(CONV_K - 1):
        acc = acc + cw_ref[j:j + 1, :] * cst[j:j + 1, :]
    acc = acc + cw_ref[CONV_K - 1:CONV_K, :] * xbc
    xc = _silu(acc)
    conv_ref[i] = jnp.concatenate([cst[1:CONV_K - 1, :], xbc], axis=0)

    dt = _softplus(dt_ref[i] + dtb_ref[...])
    decay = jnp.exp(dt * (-jnp.exp(alog_ref[...])))
    r8 = lax.broadcasted_iota(jnp.int32, (SUBLANES, LANES), 0)
    stacked = jnp.where(r8 == 0, jnp.broadcast_to(dt, (SUBLANES, LANES)),
                        jnp.where(r8 == 1, jnp.broadcast_to(decay, (SUBLANES, LANES)), 0.0))
    s_hi = stacked.astype(_BF16).astype(_F32)
    s_hl = jnp.concatenate([s_hi, stacked - s_hi], axis=0).astype(_BF16)
    ex = jnp.dot(s_hl, e_ref[...], preferred_element_type=_F32)
    ex = ex[0:SUBLANES] + ex[SUBLANES:2 * SUBLANES]
    dt_e, dec_e = ex[0:1], ex[1:2]
    xs_c = xc[:, 0:W]
    xdt = xs_c * dt_e
    b8 = jnp.concatenate([xc[:, W + g * N:W + (g + 1) * N] for g in range(G)], axis=0)
    c8 = jnp.concatenate([xc[:, W + GN + g * N:W + GN + (g + 1) * N] for g in range(G)], axis=0)
    gmask = _block_mask(G, W, GW)

    s_old = st_ref[i]
    y_all = lax.dot_general(c8.astype(_BF16), s_old.astype(_BF16), _NT,
                            preferred_element_type=_F32)
    y_old = _own_block(gmask, y_all)
    cb_e = _own_block(gmask, jnp.sum(c8 * b8, axis=-1, keepdims=True))
    y = dec_e * y_old + cb_e * xdt

    d_hi = dec_e.astype(_BF16).astype(_F32)
    d_lo = dec_e - d_hi
    kp = -(-3 * G // 16) * 16
    lhs = jnp.concatenate([jnp.where(gmask, d_hi, 0.0), jnp.where(gmask, d_lo, 0.0),
                           jnp.where(gmask, xdt, 0.0)], axis=0)
    lhs = _pad_rows(lhs, kp).astype(_BF16)
    ones_zeros = jnp.concatenate([jnp.ones((2 * G, N), _F32), jnp.zeros((2 * G, N), _F32)], axis=1)
    rhs = jnp.concatenate([ones_zeros, jnp.concatenate([jnp.zeros((G, N), _F32), b8], axis=1)], axis=0)
    rhs = _pad_rows(rhs, kp).astype(_BF16)
    res = lax.dot_general(lhs, rhs, _TN, preferred_element_type=_F32)
    st_out_ref[i] = s_old * res[:, 0:N] + res[:, N:2 * N]

    y = y + dskip_ref[...] * xs_c
    y = y * _silu(z_ref[i])
    ms = _own_block(gmask, jnp.sum(jnp.where(gmask, y * y, 0.0), axis=-1, keepdims=True) * (1.0 / GW))
    y_ref[i] = (y * lax.rsqrt(ms + EPS) * nw_ref[...]).astype(y_ref.dtype)


def _ssd_step_kernel(dm, bb, *refs):
    refs = refs[:14] + refs[-3:]
    for i in range(bb):
        _ssd_step_one(dm, i, *refs)


def _ssd_step(dm, layer, bb, proj3, state_conv, state_ssd, prev_out, conv_w, conv_b, dtb, alog,
              dskip_e, nw, e_mat):
    W, GN = dm.W, dm.GN
    nb = proj3.shape[0]
    par = lambda b: (layer, 0, 0)
    in_specs = [
        pl.BlockSpec((bb, 1, W), lambda b: (b, 0, 0)),
        pl.BlockSpec((bb, 1, W), lambda b: (b, 0, 1)),
        pl.BlockSpec((bb, 1, GN), lambda b: (b, 0, 2 * W // GN)),
        pl.BlockSpec((bb, 1, GN), lambda b: (b, 0, 2 * W // GN + 1)),
        pl.BlockSpec((bb, 1, LANES), lambda b: (b, 0, dm.o_dt // LANES)),
        pl.BlockSpec((None, bb, CONV_K - 1, dm.CD), lambda b: (layer, b, 0, 0)),
        pl.BlockSpec((None, bb, W, dm.N), lambda b: (layer, b, 0, 0)),
        pl.BlockSpec((None, CONV_K, dm.CD), par),
        pl.BlockSpec((None, 1, dm.CD), par),
        pl.BlockSpec((None, 1, LANES), par),
        pl.BlockSpec((None, 1, LANES), par),
        pl.BlockSpec((None, 1, W), par),
        pl.BlockSpec((None, 1, W), par),
        pl.BlockSpec((LANES, W), lambda b: (0, 0)),
    ]
    args = [proj3, proj3, proj3, proj3, proj3, state_conv, state_ssd, conv_w, conv_b, dtb, alog,
            dskip_e, nw, e_mat]
    aliases = {}
    if prev_out is not None:
        in_specs.append(pl.BlockSpec(memory_space=pl.ANY))
        args.append(prev_out)
        aliases = {len(args) - 1: 2}
    out_specs = [
        pl.BlockSpec((bb, 1, W), lambda b: (b, 0, 0)),
        pl.BlockSpec((bb, CONV_K - 1, dm.CD), lambda b: (b, 0, 0)),
        pl.BlockSpec((None, bb, W, dm.N), lambda b: (layer, b, 0, 0)),
    ]
    out_shape = [
        jax.ShapeDtypeStruct((nb, 1, W), _BF16),
        jax.ShapeDtypeStruct((nb, CONV_K - 1, dm.CD), _F32),
        jax.ShapeDtypeStruct(state_ssd.shape, _F32),
    ]
    return pl.pallas_call(
        functools.partial(_ssd_step_kernel, dm, bb),
        grid=(nb // bb,),
        in_specs=in_specs,
        out_specs=out_specs,
        out_shape=out_shape,
        input_output_aliases=aliases,
        compiler_params=_cparams("arbitrary"),
        name="ssd_step",
    )(*args)


def _ret_step_one(dm, i, q_ref, k_ref, v_ref, g_ref, cos_ref, sin_ref, gam_ref, st_ref, nw_ref,
                  y_ref, st_out_ref):
    DK, DV, RH, W, QK = dm.DK, dm.DV, dm.RH, dm.W, dm.QK
    gammas = [float(np.exp(np.float32(lg))) for lg in _ret_log_gammas(RH)]
    cos_t, sin_t = cos_ref[...], sin_ref[...]
    lane = lax.broadcasted_iota(jnp.int32, (1, QK), 1)
    first_half = jnp.bitwise_and(lane, DK - 1) < DK // 2

    def rotate_heads(x):
        partner = jnp.where(first_half, pltpu.roll(x, QK - DK // 2, axis=1), pltpu.roll(x, DK // 2, axis=1))
        return x * cos_t + partner * sin_t

    qr = rotate_heads(q_ref[i])
    kr = rotate_heads(k_ref[i]) * (DK ** -0.5)
    q16 = jnp.concatenate([qr[:, h * DK:(h + 1) * DK] for h in range(RH)], axis=0)
    k16 = jnp.concatenate([kr[:, h * DK:(h + 1) * DK] for h in range(RH)], axis=0)
    v = v_ref[i]
    hmask = _block_mask(RH, W, DV)
    kp = -(-RH // 16) * 16

    s_old = st_ref[i]
    y_all = lax.dot_general(_pad_rows(q16, kp).astype(_BF16), s_old.astype(_BF16), _NT,
                            preferred_element_type=_F32)[0:RH]
    y_old = _own_block(hmask, y_all)
    qk_e = _own_block(hmask, jnp.sum(q16 * k16, axis=-1, keepdims=True))
    y = gam_ref[...] * y_old + qk_e * v
    lhs = _pad_rows(jnp.where(hmask, v, 0.0), kp).astype(_BF16)
    outer = lax.dot_general(lhs, _pad_rows(k16, kp).astype(_BF16), _TN,
                            preferred_element_type=_F32)
    for h in range(RH):
        rows = slice(h * DV, (h + 1) * DV)
        st_out_ref[i, rows, :] = gammas[h] * st_ref[i, rows, :] + outer[rows, :]
    ms = _own_block(hmask, jnp.sum(jnp.where(hmask, y * y, 0.0), axis=-1, keepdims=True) * (1.0 / DV))
    y = y * lax.rsqrt(ms + EPS) * nw_ref[...] * _silu(g_ref[i])
    y_ref[i] = y.astype(y_ref.dtype)


def _ret_step_kernel(dm, bb, *refs):
    refs = refs[:9] + refs[-2:]
    for i in range(bb):
        _ret_step_one(dm, i, *refs)


def _ret_step(dm, layer, bb, proj3, state_ret, prev_out, cos_t, sin_t, gam_e, nw):
    W, QK = dm.W, dm.QK
    nb = proj3.shape[0]
    cst = lambda b: (0, 0)
    in_specs = [
        pl.BlockSpec((bb, 1, QK), lambda b: (b, 0, 0)),
        pl.BlockSpec((bb, 1, QK), lambda b: (b, 0, 1)),
        pl.BlockSpec((bb, 1, W), lambda b: (b, 0, 1)),
        pl.BlockSpec((bb, 1, W), lambda b: (b, 0, 2)),
        pl.BlockSpec((1, QK), cst),
        pl.BlockSpec((1, QK), cst),
        pl.BlockSpec((1, W), cst),
        pl.BlockSpec((None, bb, dm.RH * dm.DV, dm.DK), lambda b: (layer, b, 0, 0)),
        pl.BlockSpec((None, 1, W), lambda b: (layer, 0, 0)),
    ]
    args = [proj3, proj3, proj3, proj3, cos_t, sin_t, gam_e, state_ret, nw]
    aliases = {}
    if prev_out is not None:
        in_specs.append(pl.BlockSpec(memory_space=pl.ANY))
        args.append(prev_out)
        aliases = {len(args) - 1: 1}
    out_specs = [
        pl.BlockSpec((bb, 1, W), lambda b: (b, 0, 0)),
        pl.BlockSpec((None, bb, dm.RH * dm.DV, dm.DK), lambda b: (layer, b, 0, 0)),
    ]
    out_shape = [
        jax.ShapeDtypeStruct((nb, 1, W), _BF16),
        jax.ShapeDtypeStruct(state_ret.shape, _F32),
    ]
    return pl.pallas_call(
        functools.partial(_ret_step_kernel, dm, bb),
        grid=(nb // bb,),
        in_specs=in_specs,
        out_specs=out_specs,
        out_shape=out_shape,
        input_output_aliases=aliases,
        compiler_params=_cparams("arbitrary"),
        name="ret_step",
    )(*args)


def _rope_tables(dk, positions):
    half = dk // 2
    inv = 1.0 / (ROPE_BASE ** (jnp.arange(half, dtype=_F32) / half))
    ang = positions.astype(_F32)[:, None] * inv[None, :]
    cos = jnp.cos(ang)
    sin = jnp.sin(ang)
    return jnp.concatenate([cos, cos], axis=1), jnp.concatenate([-sin, sin], axis=1)


def _pad_lanes(v):
    return jnp.pad(v, ((0, 0), (0, LANES - v.shape[1])))[:, None, :]


def kernel(x_prompt, x_sample, state_conv, state_ssd, state_ret, norm_w, w_in, conv_w, conv_b, dt_bias,
           a_log, d_skip, ssd_norm_w, ret_norm_w, w_out, final_norm_w):
    bp, tp, d_model = x_prompt.shape
    bs, ts, _ = x_sample.shape
    depth = norm_w.shape[0]
    assert ts == 1 and tp % CHUNK == 0
    dm = Dims(d_model, dt_bias.shape[1], conv_w.shape[2], state_ret.shape[2], state_ret.shape[3],
              state_ret.shape[4])
    assert w_in.shape[2] == dm.IN_COLS
    W = dm.W

    tk = min(1024, d_model)
    w_in_t = jnp.swapaxes(w_in, 1, 2)
    tkt = min(2048, d_model)
    w_a = _transpose_cast_weights(w_in_t, 0, dm.NA, dm.o_dt + dm.H, tkt, dm.TN)
    w_b = _transpose_cast_weights(w_in_t, dm.o_dt + dm.H, dm.NB, None, tkt, dm.TN)
    w_o = _cast_weights(w_out, d_model, None, tk, dm.TN)

    heads = jnp.arange(LANES)[:, None]
    lanes = jnp.arange(W)[None, :]
    e_mat = (lanes // dm.P == heads).astype(_BF16)
    cos_p, sin_p = _rope_tables(dm.DK, jnp.arange(tp))
    cos_s, sin_s = _rope_tables(dm.DK, PAST_LEN + jnp.arange(ts))
    cos_s, sin_s = jnp.tile(cos_s, (1, dm.RH)), jnp.tile(sin_s, (1, dm.RH))
    gam_e = jnp.repeat(jnp.exp(jnp.asarray(_ret_log_gammas(dm.RH), _F32)), dm.DV)[None, :]
    cb3 = conv_b[:, None, :]
    dtb3, alog3 = _pad_lanes(dt_bias), _pad_lanes(a_log)
    dskip3 = jnp.repeat(d_skip, dm.P, axis=1)[:, None, :]
    snw3, rnw3 = ssd_norm_w[:, None, :], ret_norm_w[:, None, :]

    st_ssd_in = state_ssd.reshape(depth, bs, dm.H * dm.P, dm.N)
    st_ret_in = state_ret.reshape(depth, bs, dm.RH * dm.DV, dm.DK)

    xp = x_prompt.reshape(bp * tp, d_model)
    xs = x_sample.reshape(bs * ts, d_model)
    mp = xp.shape[0]
    tm_p = min(1024, mp)
    tm_o, tn_o = min(1024, mp), 256
    tm_n = min(256, mp)
    tn_s = _largest_divisor((1536, 1024, 512), dm.NA, dm.NB)
    tn_so = _largest_divisor((1024, 512), d_model)
    bb = 2 if bs % 2 == 0 else 1

    conv_p, ssd_p, ret_p, conv_s = [], [], [], []
    ssd_s = None
    ret_s = None
    for l in range(depth):
        h = _rmsnorm(xp, norm_w[l], _BF16, tm_n)
        proj_a = _inproj(h, w_a, l, tm_p, dm.TN)
        proj_b = _inproj(h, w_b, l, tm_p, dm.TN)
        y_ssd, c_new, s_new = _ssd_prompt(dm, l, proj_a, bp, conv_w, cb3, dtb3, alog3, dskip3, snw3, e_mat)
        y_ret, r_new = _ret_prompt(dm, l, proj_b, bp, cos_p, sin_p, rnw3)
        xp = _outproj(y_ssd, y_ret, w_o, l, xp, tm_o, tn_o)
        conv_p.append(c_new)
        ssd_p.append(s_new.reshape(bp, dm.H, dm.P, dm.N))
        ret_p.append(r_new.reshape(bp, dm.RH, dm.DV, dm.DK))

        h = _rmsnorm(xs, norm_w[l], _BF16, bs)
        proj3_a = _inproj(h, w_a, l, bs, tn_s).reshape(bs, 1, dm.NA)
        proj3_b = _inproj(h, w_b, l, bs, tn_s).reshape(bs, 1, dm.NB)
        y_ssd, c_new, ssd_s = _ssd_step(dm, l, bb, proj3_a, state_conv, st_ssd_in, ssd_s, conv_w, cb3,
                                        dtb3, alog3, dskip3, snw3, e_mat)
        y_ret, ret_s = _ret_step(dm, l, bb, proj3_b, st_ret_in, ret_s, cos_s, sin_s, gam_e, rnw3)
        xs = _outproj(y_ssd.reshape(bs, W), y_ret.reshape(bs, W), w_o, l, xs, bs, tn_so)
        conv_s.append(c_new)

    y_prompt = _rmsnorm(xp, final_norm_w, _F32, tm_n).reshape(bp, tp, d_model)
    y_sample = _rmsnorm(xs, final_norm_w, _F32, bs).reshape(bs, ts, d_model)
    return (y_prompt, y_sample, jnp.stack(conv_p), jnp.stack(ssd_p), jnp.stack(ret_p),
            jnp.stack(conv_s),
            ssd_s.reshape(depth, bs, dm.H, dm.P, dm.N),
            ret_s.reshape(depth, bs, dm.RH, dm.DV, dm.DK))
```

```python
import functools

import numpy as np
import jax
import jax.numpy as jnp
from jax import lax
from jax.experimental import pallas as pl
from jax.experimental.pallas import tpu as pltpu

SSD_HEADDIM = 64
SSD_STATE = 128
CONV_K = 4
CHUNK = 128
ROPE_BASE = 10000.0
EPS = 1e-5
PAST_LEN = 16384

LANES = 128
SUBLANES = 8
VMEM_LIMIT_BYTES = 56 * 1024 * 1024

_F32 = jnp.float32
_BF16 = jnp.bfloat16
_NT = (((1,), (1,)), ((), ()))
_TN = (((0,), (0,)), ((), ()))


def _cparams(*sem):
    return pltpu.CompilerParams(dimension_semantics=sem, vmem_limit_bytes=VMEM_LIMIT_BYTES)


def _silu(x):
    h = 0.5 * x
    return h + h * jnp.tanh(h)


def _softplus(x):
    return jnp.maximum(x, 0.0) + jnp.log1p(jnp.exp(-jnp.abs(x)))


def _split2(x):
    hi = x.astype(_BF16)
    lo = (x - hi.astype(_F32)).astype(_BF16)
    return hi, lo


def _split3(x):
    hi = x.astype(_BF16)
    r = x - hi.astype(_F32)
    mid = r.astype(_BF16)
    lo = (r - mid.astype(_F32)).astype(_BF16)
    return hi, mid, lo


def _largest_divisor(candidates, *values):
    for c in candidates:
        if all(v % c == 0 for v in values):
            return c
    raise ValueError((candidates, values))


class Dims:
    def __init__(self, d_model, n_heads, conv_dim, ret_heads, ret_dv, ret_dk):
        self.D = d_model
        self.W = d_model
        self.H = n_heads
        self.P = SSD_HEADDIM
        self.N = SSD_STATE
        self.GN = (conv_dim - self.W) // 2
        self.G = self.GN // self.N
        self.HPG = self.H // self.G
        self.GW = self.HPG * self.P
        self.CD = conv_dim
        self.RH = ret_heads
        self.DV = ret_dv
        self.DK = ret_dk
        self.QK = ret_heads * ret_dk
        self.NA = 2 * self.W + 2 * self.GN
        self.NB = 2 * self.QK + 2 * self.W
        self.IN_COLS = self.NA + self.H + self.NB
        assert self.H < LANES and self.P * 2 == LANES and self.N == LANES
        assert self.DK == LANES and self.DV == 2 * LANES and 2 * self.QK == self.W
        assert self.GW % LANES == 0 and self.W % self.GW == 0
        assert (2 * self.W) % self.GN == 0


def _cast_kernel(w_ref, o_ref):
    o_ref[...] = w_ref[...].astype(o_ref.dtype)


def _cast_weights(w, layer, tk, tn):
    _, k, n = w.shape
    return pl.pallas_call(
        _cast_kernel,
        grid=(k // tk, n // tn),
        in_specs=[pl.BlockSpec((None, tk, tn), lambda i, j: (layer, i, j))],
        out_specs=pl.BlockSpec((tk, tn), lambda i, j: (i, j)),
        out_shape=jax.ShapeDtypeStruct((k, n), _BF16),
        compiler_params=_cparams("parallel", "parallel"),
        name="cast_weights",
    )(w)


def _transpose_cast_kernel(row_off, limit, wt_ref, o_ref):
    x = wt_ref[0]
    if limit is not None:
        row = (row_off + pl.program_id(1) * x.shape[0]
               + lax.broadcasted_iota(jnp.int32, x.shape, 0))
        x = jnp.where(row < limit, x, 0.0)
    o_ref[...] = x.T.astype(o_ref.dtype)


def _transposed_rows_spec(layer, rows, k, row_index):
    return pl.BlockSpec((pl.Element(1), pl.Element(rows), pl.Element(k)),
                        lambda *ids: (layer, pl.multiple_of(row_index(*ids), SUBLANES), 0))


def _transpose_cast_weights(wt, layer, row_off, n_out, limit, tn):
    k = wt.shape[2]
    return pl.pallas_call(
        functools.partial(_transpose_cast_kernel, row_off, limit),
        grid=(1, n_out // tn),
        in_specs=[_transposed_rows_spec(layer, tn, k, lambda i, j: row_off + j * tn)],
        out_specs=pl.BlockSpec((k, tn), lambda i, j: (0, j)),
        out_shape=jax.ShapeDtypeStruct((k, n_out), _BF16),
        compiler_params=_cparams("parallel", "parallel"),
        name="transpose_cast_weights",
    )(wt)


def _norm_kernel(x_ref, w_ref, o_ref):
    x = x_ref[...]
    ms = jnp.mean(x * x, axis=-1, keepdims=True)
    o_ref[...] = (x * lax.rsqrt(ms + EPS) * w_ref[...]).astype(o_ref.dtype)


def _rmsnorm(x, w, out_dtype, tm):
    m, d = x.shape
    return pl.pallas_call(
        _norm_kernel,
        grid=(m // tm,),
        in_specs=[pl.BlockSpec((tm, d), lambda i: (i, 0)),
                  pl.BlockSpec((1, d), lambda i: (0, 0))],
        out_specs=pl.BlockSpec((tm, d), lambda i: (i, 0)),
        out_shape=jax.ShapeDtypeStruct((m, d), out_dtype),
        compiler_params=_cparams("parallel"),
        name="rmsnorm",
    )(x, w.reshape(1, d))


def _inproj_kernel(has_nxt, a_ref, w_ref, *rest):
    if has_nxt:
        wn_ref, o_ref, wo_ref = rest
        wo_ref[...] = wn_ref[0].T.astype(wo_ref.dtype)
    else:
        (o_ref,) = rest
    o_ref[...] = jnp.dot(a_ref[...], w_ref[...], preferred_element_type=_F32)


def _inproj(a, w, tm, tn, nxt=None):
    m, k = a.shape
    n = w.shape[1]
    ni, nj = m // tm, n // tn
    in_specs = [pl.BlockSpec((tm, k), lambda i, j: (i, 0)),
                pl.BlockSpec((k, tn), lambda i, j: (0, j))]
    out_specs = [pl.BlockSpec((tm, tn), lambda i, j: (i, j))]
    out_shape = [jax.ShapeDtypeStruct((m, n), _F32)]
    args = [a, w]
    if nxt is not None:
        wt, layer, row_off, n_out = nxt
        rows = n_out // (ni * nj)
        assert rows * ni * nj == n_out and rows % LANES == 0
        in_specs.append(_transposed_rows_spec(layer, rows, k, lambda i, j: row_off + (i * nj + j) * rows))
        out_specs.append(pl.BlockSpec((k, rows), lambda i, j: (0, i * nj + j)))
        out_shape.append(jax.ShapeDtypeStruct((k, n_out), _BF16))
        args.append(wt)
    res = pl.pallas_call(
        functools.partial(_inproj_kernel, nxt is not None),
        grid=(ni, nj),
        in_specs=in_specs,
        out_specs=out_specs,
        out_shape=out_shape,
        compiler_params=_cparams("arbitrary", "arbitrary"),
        name="inproj",
    )(*args)
    return res if nxt is not None else (res[0], None)


def _outproj_kernel(has_nxt, a1_ref, a2_ref, w1_ref, w2_ref, x_ref, *rest):
    if has_nxt:
        wn_ref, o_ref, wo_ref = rest
        wo_ref[...] = wn_ref[...].astype(wo_ref.dtype)
    else:
        (o_ref,) = rest
    acc = jnp.dot(a1_ref[...], w1_ref[...], preferred_element_type=_F32)
    acc = acc + jnp.dot(a2_ref[...], w2_ref[...], preferred_element_type=_F32)
    o_ref[...] = x_ref[...] + acc


def _outproj(a1, a2, w, x, tm, tn, nxt=None):
    m, k = a1.shape
    n = w.shape[1]
    ni, nj = m // tm, n // tn
    in_specs = [pl.BlockSpec((tm, k), lambda i, j: (i, 0)),
                pl.BlockSpec((tm, k), lambda i, j: (i, 0)),
                pl.BlockSpec((k, tn), lambda i, j: (0, j)),
                pl.BlockSpec((k, tn), lambda i, j: (1, j)),
                pl.BlockSpec((tm, tn), lambda i, j: (i, j))]
    out_specs = [pl.BlockSpec((tm, tn), lambda i, j: (i, j))]
    out_shape = [jax.ShapeDtypeStruct((m, n), _F32)]
    args = [a1, a2, w, w, x]
    if nxt is not None:
        wf, layer = nxt
        rows = 2 * k // (ni * nj)
        assert rows * ni * nj == 2 * k and rows % (2 * SUBLANES) == 0
        in_specs.append(pl.BlockSpec((None, rows, n), lambda i, j: (layer, i * nj + j, 0)))
        out_specs.append(pl.BlockSpec((rows, n), lambda i, j: (i * nj + j, 0)))
        out_shape.append(jax.ShapeDtypeStruct((2 * k, n), _BF16))
        args.append(wf)
    res = pl.pallas_call(
        functools.partial(_outproj_kernel, nxt is not None),
        grid=(ni, nj),
        in_specs=in_specs,
        out_specs=out_specs,
        out_shape=out_shape,
        compiler_params=_cparams("arbitrary", "arbitrary"),
        name="outproj",
    )(*args)
    return res if nxt is not None else (res[0], None)


def _conv_slabs(dm, in_refs, tail_ref, cw_ref, cb_ref, xc_ref):
    L = CHUNK
    col0 = 0
    for ref in in_refs:
        width = ref.shape[1]
        slab = _largest_divisor((512, 256, LANES), width)
        rowi = lax.broadcasted_iota(jnp.int32, (SUBLANES, slab), 0)
        for s in range(width // slab):
            src = slice(s * slab, (s + 1) * slab)
            dst = slice(col0 + s * slab, col0 + (s + 1) * slab)
            wts = [jnp.broadcast_to(cw_ref[j:j + 1, dst], (SUBLANES, slab)) for j in range(CONV_K)]
            bias = jnp.broadcast_to(cb_ref[:, dst], (SUBLANES, slab))
            prev = tail_ref[:, dst]
            prev_rolled = [pltpu.roll(prev, sh, axis=0) for sh in range(1, CONV_K)]
            for i in range(L // SUBLANES):
                cur = ref[i * SUBLANES:(i + 1) * SUBLANES, src]
                cur_rolled = [pltpu.roll(cur, sh, axis=0) for sh in range(1, CONV_K)]
                acc = bias
                for j in range(CONV_K - 1):
                    sh = CONV_K - 1 - j
                    tap = jnp.where(rowi >= sh, cur_rolled[sh - 1], prev_rolled[sh - 1])
                    acc = acc + wts[j] * tap
                acc = acc + wts[CONV_K - 1] * cur
                xc_ref[i * SUBLANES:(i + 1) * SUBLANES, dst] = _silu(acc)
                prev_rolled = cur_rolled
            tail_ref[:, dst] = ref[L - SUBLANES:L, src]
        col0 += width


def _ssd_prompt_kernel(dm, z_ref, xs_ref, b_ref, c_ref, dt_ref, cw_ref, cb_ref, dtb_ref, alog_ref,
                       dskip_ref, nw_ref, e_ref, y_ref, conv_ref, st_out_ref, tail_ref, xc_ref, st_ref):
    L, W, GN, GW, N = CHUNK, dm.W, dm.GN, dm.GW, dm.N
    c = pl.program_id(1)
    nc = pl.num_programs(1)

    @pl.when(c == 0)
    def _():
        tail_ref[...] = jnp.zeros_like(tail_ref)
        st_ref[...] = jnp.zeros_like(st_ref)

    _conv_slabs(dm, (xs_ref, b_ref, c_ref), tail_ref, cw_ref, cb_ref, xc_ref)
    keep = CONV_K - 1
    conv_ref[0, :, 0:W] = xs_ref[L - keep:L, :]
    conv_ref[0, :, W:W + GN] = b_ref[L - keep:L, :]
    conv_ref[0, :, W + GN:W + 2 * GN] = c_ref[L - keep:L, :]

    dt = _softplus(dt_ref[...] + dtb_ref[...])
    a_neg = -jnp.exp(alog_ref[...])
    la = dt * a_neg
    row = lax.broadcasted_iota(jnp.int32, (L, L), 0)
    col = lax.broadcasted_iota(jnp.int32, (L, L), 1)
    causal = row >= col
    upper = (row <= col).astype(_BF16)
    cum_t = None
    for part in _split3(la.T):
        d = jnp.dot(part, upper, preferred_element_type=_F32)
        cum_t = d if cum_t is None else cum_t + d
    cum = cum_t.T
    ecum = jnp.exp(cum)
    to_end = jnp.exp(cum[L - 1:L, :] - cum)
    stacked = jnp.concatenate([dt, ecum, to_end], axis=0)
    s_hi, s_lo = _split2(stacked)
    lane = lax.broadcasted_iota(jnp.int32, (L, LANES), 1)
    low_half = lane < dm.P

    for g in range(dm.G):
        sl = slice(g * GW, (g + 1) * GW)
        e_g = e_ref[:, sl]
        ex = (jnp.dot(s_hi, e_g, preferred_element_type=_F32)
              + jnp.dot(s_lo, e_g, preferred_element_type=_F32))
        dt_e, ecum_e, toe_e = ex[0:L], ex[L:2 * L], ex[2 * L:3 * L]
        xs_g = xc_ref[:, sl]
        xdt = xs_g * dt_e
        b_g = xc_ref[:, W + g * N:W + (g + 1) * N]
        c_g = xc_ref[:, W + GN + g * N:W + GN + (g + 1) * N]
        b_bf = b_g.astype(_BF16)
        c_bf = c_g.astype(_BF16)
        scores = lax.dot_general(c_bf, b_bf, _NT, preferred_element_type=_F32)
        st_g = st_ref[:, sl]
        y_parts = []
        for j in range(GW // LANES):
            xpair = xdt[:, j * LANES:(j + 1) * LANES].astype(_BF16)
            acc = None
            for which in range(2):
                h = g * dm.HPG + 2 * j + which
                seg = cum[:, h:h + 1] - cum_t[h:h + 1, :]
                decay = jnp.exp(jnp.where(causal, seg, -jnp.inf))
                m_h = (scores * decay).astype(_BF16)
                keep_lanes = low_half if which == 0 else jnp.logical_not(low_half)
                x_h = jnp.where(keep_lanes, xpair, jnp.zeros_like(xpair))
                d = jnp.dot(m_h, x_h, preferred_element_type=_F32)
                acc = d if acc is None else acc + d
            y_parts.append(acc)
        y = jnp.concatenate(y_parts, axis=1)
        y = y + jnp.dot(c_bf, st_g.astype(_BF16), preferred_element_type=_F32) * ecum_e
        w_g = (xdt * toe_e).astype(_BF16)
        cs = lax.dot_general(b_bf, w_g, _TN, preferred_element_type=_F32)
        st_ref[:, sl] = st_g * ecum_e[L - 1:L, :] + cs
        y = y + dskip_ref[:, sl] * xs_g
        y = y * _silu(z_ref[:, sl])
        ms = jnp.mean(y * y, axis=-1, keepdims=True)
        y_ref[:, sl] = (y * lax.rsqrt(ms + EPS) * nw_ref[:, sl]).astype(y_ref.dtype)

    @pl.when(c == nc - 1)
    def _():
        for j in range(W // LANES):
            st_out_ref[0, j * LANES:(j + 1) * LANES, :] = st_ref[:, j * LANES:(j + 1) * LANES].T


def _ssd_prompt(dm, layer, proj_a, proj_dt, nb, conv_w, conv_b, dtb, alog, dskip_e, nw, e_mat):
    L, W, GN = CHUNK, dm.W, dm.GN
    m = proj_a.shape[0]
    nc = m // nb // L
    rowblk = lambda b, c: b * nc + c
    par = lambda b, c: (layer, 0, 0)
    in_specs = [
        pl.BlockSpec((L, W), lambda b, c: (rowblk(b, c), 0)),
        pl.BlockSpec((L, W), lambda b, c: (rowblk(b, c), 1)),
        pl.BlockSpec((L, GN), lambda b, c: (rowblk(b, c), 2 * W // GN)),
        pl.BlockSpec((L, GN), lambda b, c: (rowblk(b, c), 2 * W // GN + 1)),
        pl.BlockSpec((L, LANES), lambda b, c: (rowblk(b, c), 0)),
        pl.BlockSpec((None, CONV_K, dm.CD), par),
        pl.BlockSpec((None, 1, dm.CD), par),
        pl.BlockSpec((None, 1, LANES), par),
        pl.BlockSpec((None, 1, LANES), par),
        pl.BlockSpec((None, 1, W), par),
        pl.BlockSpec((None, 1, W), par),
        pl.BlockSpec((LANES, W), lambda b, c: (0, 0)),
    ]
    out_specs = [
        pl.BlockSpec((L, W), lambda b, c: (rowblk(b, c), 0)),
        pl.BlockSpec((1, CONV_K - 1, dm.CD), lambda b, c: (b, 0, 0)),
        pl.BlockSpec((1, W, dm.N), lambda b, c: (b, 0, 0)),
    ]
    out_shape = [
        jax.ShapeDtypeStruct((m, W), _BF16),
        jax.ShapeDtypeStruct((nb, CONV_K - 1, dm.CD), _F32),
        jax.ShapeDtypeStruct((nb, W, dm.N), _F32),
    ]
    return pl.pallas_call(
        functools.partial(_ssd_prompt_kernel, dm),
        grid=(nb, nc),
        in_specs=in_specs,
        out_specs=out_specs,
        out_shape=out_shape,
        scratch_shapes=[pltpu.VMEM((SUBLANES, dm.CD), _F32),
                        pltpu.VMEM((L, dm.CD), _F32),
                        pltpu.VMEM((dm.N, W), _F32)],
        compiler_params=_cparams("arbitrary", "arbitrary"),
        name="ssd_prompt",
    )(proj_a, proj_a, proj_a, proj_a, proj_dt, conv_w, conv_b, dtb, alog, dskip_e, nw, e_mat)


def _ret_log_gammas(n):
    return [float(np.log1p(-np.exp2(np.float32(-5.0 - h)))) for h in range(n)]


def _rotate(x, cos2, sin2):
    return x * cos2 + pltpu.roll(x, LANES // 2, axis=1) * sin2


def _ret_prompt_kernel(dm, q_ref, k_ref, v_ref, g_ref, cos_ref, sin_ref, nw_ref,
                       y_ref, st_out_ref, dec_ref, ec_ref, te_ref, st_ref):
    L, DK, DV = CHUNK, dm.DK, dm.DV
    b = pl.program_id(0)
    c = pl.program_id(1)
    nc = pl.num_programs(1)
    lgs = _ret_log_gammas(dm.RH)

    @pl.when(jnp.logical_and(b == 0, c == 0))
    def _():
        row = lax.broadcasted_iota(jnp.int32, (L, L), 0)
        col = lax.broadcasted_iota(jnp.int32, (L, L), 1)
        diff = (row - col).astype(_F32)
        rowf = row.astype(_F32)
        for h in range(dm.RH):
            dec_ref[h] = jnp.exp(jnp.where(row >= col, diff * lgs[h], -jnp.inf))
            ec_ref[h] = jnp.exp((rowf + 1.0) * lgs[h])
            te_ref[h] = jnp.exp((float(L - 1) - rowf) * lgs[h])

    @pl.when(c == 0)
    def _():
        st_ref[...] = jnp.zeros_like(st_ref)

    cos2 = cos_ref[...]
    sin2 = sin_ref[...]
    scale = DK ** -0.5
    for h in range(dm.RH):
        qs = slice(h * DK, (h + 1) * DK)
        vs = slice(h * DV, (h + 1) * DV)
        qr = _rotate(q_ref[:, qs], cos2, sin2)
        kr = _rotate(k_ref[:, qs], cos2, sin2) * scale
        q_bf = qr.astype(_BF16)
        k_bf = kr.astype(_BF16)
        v_h = v_ref[:, vs]
        scores = lax.dot_general(q_bf, k_bf, _NT, preferred_element_type=_F32)
        m_h = (scores * dec_ref[h]).astype(_BF16)
        st_h = st_ref[h]
        ec = ec_ref[h]
        te = te_ref[h]
        y = jnp.dot(m_h, v_h.astype(_BF16), preferred_element_type=_F32)
        y_inter = jnp.dot(q_bf, st_h.astype(_BF16), preferred_element_type=_F32)
        y = y + y_inter * jnp.concatenate([ec, ec], axis=1)
        vt = (v_h * jnp.concatenate([te, te], axis=1)).astype(_BF16)
        cs = lax.dot_general(k_bf, vt, _TN, preferred_element_type=_F32)
        st_ref[h] = st_h * ec[L - 1:L, 0:1] + cs
        ms = jnp.mean(y * y, axis=-1, keepdims=True)
        y = y * lax.rsqrt(ms + EPS) * nw_ref[:, vs] * _silu(g_ref[:, vs])
        y_ref[:, vs] = y.astype(y_ref.dtype)

    @pl.when(c == nc - 1)
    def _():
        for h in range(dm.RH):
            st_out_ref[0, h * DV:(h + 1) * DV, :] = st_ref[h].T


def _ret_prompt(dm, layer, proj_b, nb, cos2, sin2, nw):
    L, W, QK = CHUNK, dm.W, dm.QK
    m = proj_b.shape[0]
    nc = m // nb // L
    rowblk = lambda b, c: b * nc + c
    in_specs = [
        pl.BlockSpec((L, QK), lambda b, c: (rowblk(b, c), 0)),
        pl.BlockSpec((L, QK), lambda b, c: (rowblk(b, c), 1)),
        pl.BlockSpec((L, W), lambda b, c: (rowblk(b, c), 1)),
        pl.BlockSpec((L, W), lambda b, c: (rowblk(b, c), 2)),
        pl.BlockSpec((L, LANES), lambda b, c: (c, 0)),
        pl.BlockSpec((L, LANES), lambda b, c: (c, 0)),
        pl.BlockSpec((None, 1, W), lambda b, c: (layer, 0, 0)),
    ]
    out_specs = [
        pl.BlockSpec((L, W), lambda b, c: (rowblk(b, c), 0)),
        pl.BlockSpec((1, dm.RH * dm.DV, dm.DK), lambda b, c: (b, 0, 0)),
    ]
    out_shape = [
        jax.ShapeDtypeStruct((m, W), _BF16),
        jax.ShapeDtypeStruct((nb, dm.RH * dm.DV, dm.DK), _F32),
    ]
    return pl.pallas_call(
        functools.partial(_ret_prompt_kernel, dm),
        grid=(nb, nc),
        in_specs=in_specs,
        out_specs=out_specs,
        out_shape=out_shape,
        scratch_shapes=[pltpu.VMEM((dm.RH, L, L), _F32),
                        pltpu.VMEM((dm.RH, L, LANES), _F32),
                        pltpu.VMEM((dm.RH, L, LANES), _F32),
                        pltpu.VMEM((dm.RH, dm.DK, dm.DV), _F32)],
        compiler_params=_cparams("arbitrary", "arbitrary"),
        name="ret_prompt",
    )(proj_b, proj_b, proj_b, proj_b, cos2, sin2, nw)


def _block_mask(rows, width, block):
    lane = lax.broadcasted_iota(jnp.int32, (rows, width), 1)
    lo = lax.broadcasted_iota(jnp.int32, (rows, width), 0) * block
    return jnp.logical_and(lane >= lo, lane < lo + block)


def _own_block(mask, x):
    return jnp.sum(jnp.where(mask, x, 0.0), axis=0, keepdims=True)


def _pad_rows(x, rows):
    if x.shape[0] == rows:
        return x
    return jnp.concatenate([x, jnp.zeros((rows - x.shape[0], x.shape[1]), x.dtype)], axis=0)


def _ssd_step_one(dm, i, z_ref, xs_ref, b_ref, c_ref, dt_ref, cst_ref, st_ref, cw_ref, cb_ref,
                  dtb_ref, alog_ref, dskip_ref, nw_ref, e_ref, y_ref, conv_ref, st_out_ref):
    W, GN, GW, N, G = dm.W, dm.GN, dm.GW, dm.N, dm.G
    xbc = jnp.concatenate([xs_ref[i], b_ref[i], c_ref[i]], axis=1)
    cst = cst_ref[i]
    acc = cb_ref[...]
    for j in range(CONV_K - 1):
        acc = acc + cw_ref[j:j + 1, :] * cst[j:j + 1, :]
    acc = acc + cw_ref[CONV_K - 1:CONV_K, :] * xbc
    xc = _silu(acc)
    conv_ref[i] = jnp.concatenate([cst[1:CONV_K - 1, :], xbc], axis=0)

    dt = _softplus(dt_ref[i] + dtb_ref[...])
    decay = jnp.exp(dt * (-jnp.exp(alog_ref[...])))
    r8 = lax.broadcasted_iota(jnp.int32, (SUBLANES, LANES), 0)
    stacked = jnp.where(r8 == 0, jnp.broadcast_to(dt, (SUBLANES, LANES)),
                        jnp.where(r8 == 1, jnp.broadcast_to(decay, (SUBLANES, LANES)), 0.0))
    s_hi = stacked.astype(_BF16).astype(_F32)
    s_hl = jnp.concatenate([s_hi, stacked - s_hi], axis=0).astype(_BF16)
    ex = jnp.dot(s_hl, e_ref[...], preferred_element_type=_F32)
    ex = ex[0:SUBLANES] + ex[SUBLANES:2 * SUBLANES]
    dt_e, dec_e = ex[0:1], ex[1:2]
    xs_c = xc[:, 0:W]
    xdt = xs_c * dt_e
    b8 = jnp.concatenate([xc[:, W + g * N:W + (g + 1) * N] for g in range(G)], axis=0)
    c8 = jnp.concatenate([xc[:, W + GN + g * N:W + GN + (g + 1) * N] for g in range(G)], axis=0)
    gmask = _block_mask(G, W, GW)

    s_old = st_ref[i]
    y_all = lax.dot_general(c8.astype(_BF16), s_old.astype(_BF16), _NT,
                            preferred_element_type=_F32)
    y_old = _own_block(gmask, y_all)
    cb_e = _own_block(gmask, jnp.sum(c8 * b8, axis=-1, keepdims=True))
    y = dec_e * y_old + cb_e * xdt

    d_hi = dec_e.astype(_BF16).astype(_F32)
    d_lo = dec_e - d_hi
    kp = -(-3 * G // 16) * 16
    lhs = jnp.concatenate([jnp.where(gmask, d_hi, 0.0), jnp.where(gmask, d_lo, 0.0),
                           jnp.where(gmask, xdt, 0.0)], axis=0)
    lhs = _pad_rows(lhs, kp).astype(_BF16)
    ones_zeros = jnp.concatenate([jnp.ones((2 * G, N), _F32), jnp.zeros((2 * G, N), _F32)], axis=1)
    rhs = jnp.concatenate([ones_zeros, jnp.concatenate([jnp.zeros((G, N), _F32), b8], axis=1)], axis=0)
    rhs = _pad_rows(rhs, kp).astype(_BF16)
    res = lax.dot_general(lhs, rhs, _TN, preferred_element_type=_F32)
    st_out_ref[i] = s_old * res[:, 0:N] + res[:, N:2 * N]

    y = y + dskip_ref[...] * xs_c
    y = y * _silu(z_ref[i])
    ms = _own_block(gmask, jnp.sum(jnp.where(gmask, y * y, 0.0), axis=-1, keepdims=True) * (1.0 / GW))
    y_ref[i] = (y * lax.rsqrt(ms + EPS) * nw_ref[...]).astype(y_ref.dtype)


def _ssd_step_kernel(dm, bb, *refs):
    refs = refs[:14] + refs[-3:]
    for i in range(bb):
        _ssd_step_one(dm, i, *refs)


def _ssd_step(dm, layer, bb, proj3, proj3_dt, state_conv, state_ssd, prev_out, conv_w, conv_b, dtb, alog,
              dskip_e, nw, e_mat):
    W, GN = dm.W, dm.GN
    nb = proj3.shape[0]
    par = lambda b: (layer, 0, 0)
    in_specs = [
        pl.BlockSpec((bb, 1, W), lambda b: (b, 0, 0)),
        pl.BlockSpec((bb, 1, W), lambda b: (b, 0, 1)),
        pl.BlockSpec((bb, 1, GN), lambda b: (b, 0, 2 * W // GN)),
        pl.BlockSpec((bb, 1, GN), lambda b: (b, 0, 2 * W // GN + 1)),
        pl.BlockSpec((bb, 1, LANES), lambda b: (b, 0, 0)),
        pl.BlockSpec((None, bb, CONV_K - 1, dm.CD), lambda b: (layer, b, 0, 0)),
        pl.BlockSpec((None, bb, W, dm.N), lambda b: (layer, b, 0, 0)),
        pl.BlockSpec((None, CONV_K, dm.CD), par),
        pl.BlockSpec((None, 1, dm.CD), par),
        pl.BlockSpec((None, 1, LANES), par),
        pl.BlockSpec((None, 1, LANES), par),
        pl.BlockSpec((None, 1, W), par),
        pl.BlockSpec((None, 1, W), par),
        pl.BlockSpec((LANES, W), lambda b: (0, 0)),
    ]
    args = [proj3, proj3, proj3, proj3, proj3_dt, state_conv, state_ssd, conv_w, conv_b, dtb, alog,
            dskip_e, nw, e_mat]
    aliases = {}
    if prev_out is not None:
        in_specs.append(pl.BlockSpec(memory_space=pl.ANY))
        args.append(prev_out)
        aliases = {len(args) - 1: 2}
    out_specs = [
        pl.BlockSpec((bb, 1, W), lambda b: (b, 0, 0)),
        pl.BlockSpec((bb, CONV_K - 1, dm.CD), lambda b: (b, 0, 0)),
        pl.BlockSpec((None, bb, W, dm.N), lambda b: (layer, b, 0, 0)),
    ]
    out_shape = [
        jax.ShapeDtypeStruct((nb, 1, W), _BF16),
        jax.ShapeDtypeStruct((nb, CONV_K - 1, dm.CD), _F32),
        jax.ShapeDtypeStruct(state_ssd.shape, _F32),
    ]
    return pl.pallas_call(
        functools.partial(_ssd_step_kernel, dm, bb),
        grid=(nb // bb,),
        in_specs=in_specs,
        out_specs=out_specs,
        out_shape=out_shape,
        input_output_aliases=aliases,
        compiler_params=_cparams("arbitrary"),
        name="ssd_step",
    )(*args)


def _ret_step_one(dm, i, q_ref, k_ref, v_ref, g_ref, cos_ref, sin_ref, gam_ref, st_ref, nw_ref,
                  y_ref, st_out_ref):
    DK, DV, RH, W, QK = dm.DK, dm.DV, dm.RH, dm.W, dm.QK
    gammas = [float(np.exp(np.float32(lg))) for lg in _ret_log_gammas(RH)]
    cos_t, sin_t = cos_ref[...], sin_ref[...]
    lane = lax.broadcasted_iota(jnp.int32, (1, QK), 1)
    first_half = jnp.bitwise_and(lane, DK - 1) < DK // 2

    def rotate_heads(x):
        partner = jnp.where(first_half, pltpu.roll(x, QK - DK // 2, axis=1), pltpu.roll(x, DK // 2, axis=1))
        return x * cos_t + partner * sin_t

    qr = rotate_heads(q_ref[i])
    kr = rotate_heads(k_ref[i]) * (DK ** -0.5)
    q16 = jnp.concatenate([qr[:, h * DK:(h + 1) * DK] for h in range(RH)], axis=0)
    k16 = jnp.concatenate([kr[:, h * DK:(h + 1) * DK] for h in range(RH)], axis=0)
    v = v_ref[i]
    hmask = _block_mask(RH, W, DV)
    kp = -(-RH // 16) * 16

    s_old = st_ref[i]
    y_all = lax.dot_general(_pad_rows(q16, kp).astype(_BF16), s_old.astype(_BF16), _NT,
                            preferred_element_type=_F32)[0:RH]
    y_old = _own_block(hmask, y_all)
    qk_e = _own_block(hmask, jnp.sum(q16 * k16, axis=-1, keepdims=True))
    y = gam_ref[...] * y_old + qk_e * v
    lhs = _pad_rows(jnp.where(hmask, v, 0.0), kp).astype(_BF16)
    outer = lax.dot_general(lhs, _pad_rows(k16, kp).astype(_BF16), _TN,
                            preferred_element_type=_F32)
    for h in range(RH):
        rows = slice(h * DV, (h + 1) * DV)
        st_out_ref[i, rows, :] = gammas[h] * st_ref[i, rows, :] + outer[rows, :]
    ms = _own_block(hmask, jnp.sum(jnp.where(hmask, y * y, 0.0), axis=-1, keepdims=True) * (1.0 / DV))
    y = y * lax.rsqrt(ms + EPS) * nw_ref[...] * _silu(g_ref[i])
    y_ref[i] = y.astype(y_ref.dtype)


def _ret_step_kernel(dm, bb, *refs):
    refs = refs[:9] + refs[-2:]
    for i in range(bb):
        _ret_step_one(dm, i, *refs)


def _ret_step(dm, layer, bb, proj3, state_ret, prev_out, cos_t, sin_t, gam_e, nw):
    W, QK = dm.W, dm.QK
    nb = proj3.shape[0]
    cst = lambda b: (0, 0)
    in_specs = [
        pl.BlockSpec((bb, 1, QK), lambda b: (b, 0, 0)),
        pl.BlockSpec((bb, 1, QK), lambda b: (b, 0, 1)),
        pl.BlockSpec((bb, 1, W), lambda b: (b, 0, 1)),
        pl.BlockSpec((bb, 1, W), lambda b: (b, 0, 2)),
        pl.BlockSpec((1, QK), cst),
        pl.BlockSpec((1, QK), cst),
        pl.BlockSpec((1, W), cst),
        pl.BlockSpec((None, bb, dm.RH * dm.DV, dm.DK), lambda b: (layer, b, 0, 0)),
        pl.BlockSpec((None, 1, W), lambda b: (layer, 0, 0)),
    ]
    args = [proj3, proj3, proj3, proj3, cos_t, sin_t, gam_e, state_ret, nw]
    aliases = {}
    if prev_out is not None:
        in_specs.append(pl.BlockSpec(memory_space=pl.ANY))
        args.append(prev_out)
        aliases = {len(args) - 1: 1}
    out_specs = [
        pl.BlockSpec((bb, 1, W), lambda b: (b, 0, 0)),
        pl.BlockSpec((None, bb, dm.RH * dm.DV, dm.DK), lambda b: (layer, b, 0, 0)),
    ]
    out_shape = [
        jax.ShapeDtypeStruct((nb, 1, W), _BF16),
        jax.ShapeDtypeStruct(state_ret.shape, _F32),
    ]
    return pl.pallas_call(
        functools.partial(_ret_step_kernel, dm, bb),
        grid=(nb // bb,),
        in_specs=in_specs,
        out_specs=out_specs,
        out_shape=out_shape,
        input_output_aliases=aliases,
        compiler_params=_cparams("arbitrary"),
        name="ret_step",
    )(*args)


def _rope_tables(dk, positions):
    half = dk // 2
    inv = 1.0 / (ROPE_BASE ** (jnp.arange(half, dtype=_F32) / half))
    ang = positions.astype(_F32)[:, None] * inv[None, :]
    cos = jnp.cos(ang)
    sin = jnp.sin(ang)
    return jnp.concatenate([cos, cos], axis=1), jnp.concatenate([-sin, sin], axis=1)


def _pad_lanes(v):
    return jnp.pad(v, ((0, 0), (0, LANES - v.shape[1])))[:, None, :]


def kernel(x_prompt, x_sample, state_conv, state_ssd, state_ret, norm_w, w_in, conv_w, conv_b, dt_bias,
           a_log, d_skip, ssd_norm_w, ret_norm_w, w_out, final_norm_w):
    bp, tp, d_model = x_prompt.shape
    bs, ts, _ = x_sample.shape
    depth = norm_w.shape[0]
    assert ts == 1 and tp % CHUNK == 0
    dm = Dims(d_model, dt_bias.shape[1], conv_w.shape[2], state_ret.shape[2], state_ret.shape[3],
              state_ret.shape[4])
    assert w_in.shape[2] == dm.IN_COLS
    W = dm.W

    w_in_t = jnp.swapaxes(w_in, 1, 2)
    off_b = dm.NA + dm.H
    w_a = _transpose_cast_weights(w_in_t, 0, 0, dm.NA, None, 256)
    w_b = _transpose_cast_weights(w_in_t, 0, off_b, dm.NB, None, 256)
    w_dt = [_transpose_cast_weights(w_in_t, l, dm.NA, LANES, off_b, LANES) for l in range(depth)]
    w_o = _cast_weights(w_out, 0, min(1024, 2 * d_model), 512)

    heads = jnp.arange(LANES)[:, None]
    lanes = jnp.arange(W)[None, :]
    e_mat = (lanes // dm.P == heads).astype(_BF16)
    cos_p, sin_p = _rope_tables(dm.DK, jnp.arange(tp))
    cos_s, sin_s = _rope_tables(dm.DK, PAST_LEN + jnp.arange(ts))
    cos_s, sin_s = jnp.tile(cos_s, (1, dm.RH)), jnp.tile(sin_s, (1, dm.RH))
    gam_e = jnp.repeat(jnp.exp(jnp.asarray(_ret_log_gammas(dm.RH), _F32)), dm.DV)[None, :]
    cb3 = conv_b[:, None, :]
    dtb3, alog3 = _pad_lanes(dt_bias), _pad_lanes(a_log)
    dskip3 = jnp.repeat(d_skip, dm.P, axis=1)[:, None, :]
    snw3, rnw3 = ssd_norm_w[:, None, :], ret_norm_w[:, None, :]

    st_ssd_in = state_ssd.reshape(depth, bs, dm.H * dm.P, dm.N)
    st_ret_in = state_ret.reshape(depth, bs, dm.RH * dm.DV, dm.DK)

    xp = x_prompt.reshape(bp * tp, d_model)
    xs = x_sample.reshape(bs * ts, d_model)
    mp = xp.shape[0]
    tm_p = min(1024, mp)
    tm_o, tn_o = min(1024, mp), 256
    tm_n = min(256, mp)
    tn_a = _largest_divisor((1024, 512), dm.NA)
    tn_b = _largest_divisor((1024, 512), dm.NB)
    tn_s = _largest_divisor((2048, 1024, 512), dm.NA, dm.NB)
    tn_so = _largest_divisor((1024, 512), d_model)
    bb = _largest_divisor((4, 2, 1), bs)

    conv_p, ssd_p, ret_p, conv_s = [], [], [], []
    ssd_s = None
    ret_s = None
    for l in range(depth):
        more = l + 1 < depth
        h = _rmsnorm(xp, norm_w[l], _BF16, tm_n)
        proj_a, w_a_next = _inproj(h, w_a, tm_p, tn_a, (w_in_t, l + 1, 0, dm.NA) if more else None)
        proj_b, w_b_next = _inproj(h, w_b, tm_p, tn_b, (w_in_t, l + 1, off_b, dm.NB) if more else None)
        proj_dt, _ = _inproj(h, w_dt[l], tm_p, LANES)
        y_ssd, c_new, s_new = _ssd_prompt(dm, l, proj_a, proj_dt, bp, conv_w, cb3, dtb3, alog3, dskip3, snw3,
                                          e_mat)
        y_ret, r_new = _ret_prompt(dm, l, proj_b, bp, cos_p, sin_p, rnw3)
        xp, w_o_next = _outproj(y_ssd, y_ret, w_o, xp, tm_o, tn_o, (w_out, l + 1) if more else None)
        conv_p.append(c_new)
        ssd_p.append(s_new.reshape(bp, dm.H, dm.P, dm.N))
        ret_p.append(r_new.reshape(bp, dm.RH, dm.DV, dm.DK))

        h = _rmsnorm(xs, norm_w[l], _BF16, bs)
        proj3_a = _inproj(h, w_a, bs, tn_s)[0].reshape(bs, 1, dm.NA)
        proj3_b = _inproj(h, w_b, bs, tn_s)[0].reshape(bs, 1, dm.NB)
        proj3_dt = _inproj(h, w_dt[l], bs, LANES)[0].reshape(bs, 1, LANES)
        y_ssd, c_new, ssd_s = _ssd_step(dm, l, bb, proj3_a, proj3_dt, state_conv, st_ssd_in, ssd_s, conv_w, cb3,
                                        dtb3, alog3, dskip3, snw3, e_mat)
        y_ret, ret_s = _ret_step(dm, l, bb, proj3_b, st_ret_in, ret_s, cos_s, sin_s, gam_e, rnw3)
        xs, _ = _outproj(y_ssd.reshape(bs, W), y_ret.reshape(bs, W), w_o, xs, bs, tn_so)
        conv_s.append(c_new)
        w_a, w_b, w_o = w_a_next, w_b_next, w_o_next

    y_prompt = _rmsnorm(xp, final_norm_w, _F32, tm_n).reshape(bp, tp, d_model)
    y_sample = _rmsnorm(xs, final_norm_w, _F32, bs).reshape(bs, ts, d_model)
    return (y_prompt, y_sample, jnp.stack(conv_p), jnp.stack(ssd_p), jnp.stack(ret_p),
            jnp.stack(conv_s),
            ssd_s.reshape(depth, bs, dm.H, dm.P, dm.N),
            ret_s.reshape(depth, bs, dm.RH, dm.DV, dm.DK))
```

```python
import functools

import numpy as np
import jax
import jax.numpy as jnp
from jax import lax
from jax.experimental import pallas as pl
from jax.experimental.pallas import tpu as pltpu

SSD_HEADDIM = 64
SSD_STATE = 128
CONV_K = 4
CHUNK = 128
ROPE_BASE = 10000.0
EPS = 1e-5
PAST_LEN = 16384

LANES = 128
SUBLANES = 8
VMEM_LIMIT_BYTES = 56 * 1024 * 1024

_F32 = jnp.float32
_BF16 = jnp.bfloat16
_NT = (((1,), (1,)), ((), ()))
_TN = (((0,), (0,)), ((), ()))


def _cparams(*sem):
    return pltpu.CompilerParams(dimension_semantics=sem, vmem_limit_bytes=VMEM_LIMIT_BYTES)


def _silu(x):
    h = 0.5 * x
    return h + h * jnp.tanh(h)


def _softplus(x):
    return jnp.maximum(x, 0.0) + jnp.log1p(jnp.exp(-jnp.abs(x)))


def _split2(x):
    hi = x.astype(_BF16)
    lo = (x - hi.astype(_F32)).astype(_BF16)
    return hi, lo


def _split3(x):
    hi = x.astype(_BF16)
    r = x - hi.astype(_F32)
    mid = r.astype(_BF16)
    lo = (r - mid.astype(_F32)).astype(_BF16)
    return hi, mid, lo


def _largest_divisor(candidates, *values):
    for c in candidates:
        if all(v % c == 0 for v in values):
            return c
    raise ValueError((candidates, values))


class Dims:
    def __init__(self, d_model, n_heads, conv_dim, ret_heads, ret_dv, ret_dk):
        self.D = d_model
        self.W = d_model
        self.H = n_heads
        self.P = SSD_HEADDIM
        self.N = SSD_STATE
        self.GN = (conv_dim - self.W) // 2
        self.G = self.GN // self.N
        self.HPG = self.H // self.G
        self.GW = self.HPG * self.P
        self.CD = conv_dim
        self.RH = ret_heads
        self.DV = ret_dv
        self.DK = ret_dk
        self.QK = ret_heads * ret_dk
        self.NA = 2 * self.W + 2 * self.GN
        self.NB = 2 * self.QK + 2 * self.W
        self.IN_COLS = self.NA + self.H + self.NB
        assert self.H < LANES and self.P * 2 == LANES and self.N == LANES
        assert self.DK == LANES and self.DV == 2 * LANES and 2 * self.QK == self.W
        assert self.GW % LANES == 0 and self.W % self.GW == 0
        assert (2 * self.W) % self.GN == 0


def _cast_kernel(w_ref, o_ref):
    o_ref[...] = w_ref[...].astype(o_ref.dtype)


def _cast_weights(w, layer, tk, tn):
    _, k, n = w.shape
    return pl.pallas_call(
        _cast_kernel,
        grid=(k // tk, n // tn),
        in_specs=[pl.BlockSpec((None, tk, tn), lambda i, j: (layer, i, j))],
        out_specs=pl.BlockSpec((tk, tn), lambda i, j: (i, j)),
        out_shape=jax.ShapeDtypeStruct((k, n), _BF16),
        compiler_params=_cparams("parallel", "parallel"),
        name="cast_weights",
    )(w)


def _transpose_cast_kernel(row_off, limit, wt_ref, o_ref):
    x = wt_ref[0]
    if limit is not None:
        row = (row_off + pl.program_id(1) * x.shape[0]
               + lax.broadcasted_iota(jnp.int32, x.shape, 0))
        x = jnp.where(row < limit, x, 0.0)
    o_ref[...] = x.T.astype(o_ref.dtype)


def _transposed_rows_spec(layer, rows, k, row_index):
    return pl.BlockSpec((pl.Element(1), pl.Element(rows), pl.Element(k)),
                        lambda *ids: (layer, pl.multiple_of(row_index(*ids), SUBLANES), 0))


def _transpose_cast_weights(wt, layer, row_off, n_out, limit, tn):
    k = wt.shape[2]
    return pl.pallas_call(
        functools.partial(_transpose_cast_kernel, row_off, limit),
        grid=(1, n_out // tn),
        in_specs=[_transposed_rows_spec(layer, tn, k, lambda i, j: row_off + j * tn)],
        out_specs=pl.BlockSpec((k, tn), lambda i, j: (0, j)),
        out_shape=jax.ShapeDtypeStruct((k, n_out), _BF16),
        compiler_params=_cparams("parallel", "parallel"),
        name="transpose_cast_weights",
    )(wt)


def _norm_kernel(x_ref, w_ref, o_ref):
    x = x_ref[...]
    ms = jnp.mean(x * x, axis=-1, keepdims=True)
    o_ref[...] = (x * lax.rsqrt(ms + EPS) * w_ref[...]).astype(o_ref.dtype)


def _rmsnorm(x, w, out_dtype, tm):
    m, d = x.shape
    return pl.pallas_call(
        _norm_kernel,
        grid=(m // tm,),
        in_specs=[pl.BlockSpec((tm, d), lambda i: (i, 0)),
                  pl.BlockSpec((1, d), lambda i: (0, 0))],
        out_specs=pl.BlockSpec((tm, d), lambda i: (i, 0)),
        out_shape=jax.ShapeDtypeStruct((m, d), out_dtype),
        compiler_params=_cparams("parallel"),
        name="rmsnorm",
    )(x, w.reshape(1, d))


def _proj_kernel(nxt_kind, has_resid, a_ref, w_ref, *rest):
    rest = list(rest)
    x_ref = rest.pop(0) if has_resid else None
    if nxt_kind is None:
        (o_ref,) = rest
    else:
        wn_ref, o_ref, wo_ref = rest
        slab = wn_ref[0].T if nxt_kind == "transposed" else wn_ref[...]
        wo_ref[...] = slab.astype(wo_ref.dtype)
    acc = jnp.dot(a_ref[...], w_ref[...], preferred_element_type=_F32)
    o_ref[...] = acc if x_ref is None else x_ref[...] + acc


def _proj(a, w, tm, tn, nxt=None, resid=None, krow=0):
    m, k = a.shape
    n = w.shape[1]
    ni, nj = m // tm, n // tn
    in_specs = [pl.BlockSpec((tm, k), lambda i, j: (i, 0)),
                pl.BlockSpec((k, tn), lambda i, j: (krow, j))]
    args = [a, w]
    if resid is not None:
        in_specs.append(pl.BlockSpec((tm, tn), lambda i, j: (i, j)))
        args.append(resid)
    out_specs = [pl.BlockSpec((tm, tn), lambda i, j: (i, j))]
    out_shape = [jax.ShapeDtypeStruct((m, n), _F32)]
    if nxt is not None:
        kind, wsrc, layer, row_off, total = nxt
        rows = total // (ni * nj)
        assert rows * ni * nj == total
        assert rows % LANES == 0 if kind == "transposed" else (rows % (2 * SUBLANES) == 0 and row_off % rows == 0)
        if kind == "transposed":
            in_specs.append(_transposed_rows_spec(layer, rows, k, lambda i, j: row_off + (i * nj + j) * rows))
            out_specs.append(pl.BlockSpec((k, rows), lambda i, j: (0, i * nj + j)))
            out_shape.append(jax.ShapeDtypeStruct((k, total), _BF16))
        else:
            width = wsrc.shape[2]
            in_specs.append(pl.BlockSpec((None, rows, width),
                                         lambda i, j: (layer, row_off // rows + i * nj + j, 0)))
            out_specs.append(pl.BlockSpec((rows, width), lambda i, j: (i * nj + j, 0)))
            out_shape.append(jax.ShapeDtypeStruct((total, width), _BF16))
        args.append(wsrc)
    res = pl.pallas_call(
        functools.partial(_proj_kernel, None if nxt is None else nxt[0], resid is not None),
        grid=(ni, nj),
        in_specs=in_specs,
        out_specs=out_specs,
        out_shape=out_shape,
        compiler_params=_cparams("arbitrary", "arbitrary"),
        name="proj",
    )(*args)
    return res if nxt is not None else (res[0], None)


def _outproj_kernel(a1_ref, a2_ref, w1_ref, w2_ref, x_ref, o_ref):
    acc = jnp.dot(a1_ref[...], w1_ref[...], preferred_element_type=_F32)
    acc = acc + jnp.dot(a2_ref[...], w2_ref[...], preferred_element_type=_F32)
    o_ref[...] = x_ref[...] + acc


def _outproj(a1, a2, w1, w2, x, tm, tn):
    m, k = a1.shape
    n = w1[0].shape[1]
    return pl.pallas_call(
        _outproj_kernel,
        grid=(m // tm, n // tn),
        in_specs=[pl.BlockSpec((tm, k), lambda i, j: (i, 0)),
                  pl.BlockSpec((tm, k), lambda i, j: (i, 0)),
                  pl.BlockSpec((k, tn), lambda i, j: (w1[1], j)),
                  pl.BlockSpec((k, tn), lambda i, j: (w2[1], j)),
                  pl.BlockSpec((tm, tn), lambda i, j: (i, j))],
        out_specs=pl.BlockSpec((tm, tn), lambda i, j: (i, j)),
        out_shape=jax.ShapeDtypeStruct((m, n), _F32),
        compiler_params=_cparams("arbitrary", "arbitrary"),
        name="outproj",
    )(a1, a2, w1[0], w2[0], x)


def _conv_slabs(dm, in_refs, tail_ref, cw_ref, cb_ref, xc_ref):
    L = CHUNK
    col0 = 0
    for ref in in_refs:
        width = ref.shape[1]
        slab = _largest_divisor((512, 256, LANES), width)
        rowi = lax.broadcasted_iota(jnp.int32, (SUBLANES, slab), 0)
        for s in range(width // slab):
            src = slice(s * slab, (s + 1) * slab)
            dst = slice(col0 + s * slab, col0 + (s + 1) * slab)
            wts = [jnp.broadcast_to(cw_ref[j:j + 1, dst], (SUBLANES, slab)) for j in range(CONV_K)]
            bias = jnp.broadcast_to(cb_ref[:, dst], (SUBLANES, slab))
            prev = tail_ref[:, dst]
            prev_rolled = [pltpu.roll(prev, sh, axis=0) for sh in range(1, CONV_K)]
            for i in range(L // SUBLANES):
                cur = ref[i * SUBLANES:(i + 1) * SUBLANES, src]
                cur_rolled = [pltpu.roll(cur, sh, axis=0) for sh in range(1, CONV_K)]
                acc = bias
                for j in range(CONV_K - 1):
                    sh = CONV_K - 1 - j
                    tap = jnp.where(rowi >= sh, cur_rolled[sh - 1], prev_rolled[sh - 1])
                    acc = acc + wts[j] * tap
                acc = acc + wts[CONV_K - 1] * cur
                xc_ref[i * SUBLANES:(i + 1) * SUBLANES, dst] = _silu(acc)
                prev_rolled = cur_rolled
            tail_ref[:, dst] = ref[L - SUBLANES:L, src]
        col0 += width


def _ssd_prompt_kernel(dm, z_ref, xs_ref, b_ref, c_ref, h_ref, wdt_ref, cw_ref, cb_ref, dtb_ref, alog_ref,
                       dskip_ref, nw_ref, e_ref, y_ref, conv_ref, st_out_ref, tail_ref, xc_ref, st_ref):
    L, W, GN, GW, N = CHUNK, dm.W, dm.GN, dm.GW, dm.N
    c = pl.program_id(1)
    nc = pl.num_programs(1)

    @pl.when(c == 0)
    def _():
        tail_ref[...] = jnp.zeros_like(tail_ref)
        st_ref[...] = jnp.zeros_like(st_ref)

    _conv_slabs(dm, (xs_ref, b_ref, c_ref), tail_ref, cw_ref, cb_ref, xc_ref)
    keep = CONV_K - 1
    conv_ref[0, :, 0:W] = xs_ref[L - keep:L, :]
    conv_ref[0, :, W:W + GN] = b_ref[L - keep:L, :]
    conv_ref[0, :, W + GN:W + 2 * GN] = c_ref[L - keep:L, :]

    dt_raw = jnp.dot(h_ref[...], wdt_ref[...], preferred_element_type=_F32)
    dt = _softplus(dt_raw + dtb_ref[...])
    a_neg = -jnp.exp(alog_ref[...])
    la = dt * a_neg
    row = lax.broadcasted_iota(jnp.int32, (L, L), 0)
    col = lax.broadcasted_iota(jnp.int32, (L, L), 1)
    causal = row >= col
    upper = (row <= col).astype(_BF16)
    cum_t = None
    for part in _split3(la.T):
        d = jnp.dot(part, upper, preferred_element_type=_F32)
        cum_t = d if cum_t is None else cum_t + d
    cum = cum_t.T
    ecum = jnp.exp(cum)
    to_end = jnp.exp(cum[L - 1:L, :] - cum)
    stacked = jnp.concatenate([dt, ecum, to_end], axis=0)
    s_hi, s_lo = _split2(stacked)
    lane = lax.broadcasted_iota(jnp.int32, (L, LANES), 1)
    low_half = lane < dm.P

    for g in range(dm.G):
        sl = slice(g * GW, (g + 1) * GW)
        e_g = e_ref[:, sl]
        ex = (jnp.dot(s_hi, e_g, preferred_element_type=_F32)
              + jnp.dot(s_lo, e_g, preferred_element_type=_F32))
        dt_e, ecum_e, toe_e = ex[0:L], ex[L:2 * L], ex[2 * L:3 * L]
        xs_g = xc_ref[:, sl]
        xdt = xs_g * dt_e
        b_g = xc_ref[:, W + g * N:W + (g + 1) * N]
        c_g = xc_ref[:, W + GN + g * N:W + GN + (g + 1) * N]
        b_bf = b_g.astype(_BF16)
        c_bf = c_g.astype(_BF16)
        scores = lax.dot_general(c_bf, b_bf, _NT, preferred_element_type=_F32)
        st_g = st_ref[:, sl]
        y_parts = []
        for j in range(GW // LANES):
            xpair = xdt[:, j * LANES:(j + 1) * LANES].astype(_BF16)
            acc = None
            for which in range(2):
                h = g * dm.HPG + 2 * j + which
                seg = cum[:, h:h + 1] - cum_t[h:h + 1, :]
                decay = jnp.exp(jnp.where(causal, seg, -jnp.inf))
                m_h = (scores * decay).astype(_BF16)
                keep_lanes = low_half if which == 0 else jnp.logical_not(low_half)
                x_h = jnp.where(keep_lanes, xpair, jnp.zeros_like(xpair))
                d = jnp.dot(m_h, x_h, preferred_element_type=_F32)
                acc = d if acc is None else acc + d
            y_parts.append(acc)
        y = jnp.concatenate(y_parts, axis=1)
        y = y + jnp.dot(c_bf, st_g.astype(_BF16), preferred_element_type=_F32) * ecum_e
        w_g = (xdt * toe_e).astype(_BF16)
        cs = lax.dot_general(b_bf, w_g, _TN, preferred_element_type=_F32)
        st_ref[:, sl] = st_g * ecum_e[L - 1:L, :] + cs
        y = y + dskip_ref[:, sl] * xs_g
        y = y * _silu(z_ref[:, sl])
        ms = jnp.mean(y * y, axis=-1, keepdims=True)
        y_ref[:, sl] = (y * lax.rsqrt(ms + EPS) * nw_ref[:, sl]).astype(y_ref.dtype)

    @pl.when(c == nc - 1)
    def _():
        for j in range(W // LANES):
            st_out_ref[0, j * LANES:(j + 1) * LANES, :] = st_ref[:, j * LANES:(j + 1) * LANES].T


def _ssd_prompt(dm, layer, proj_a, h, w_dt, nb, conv_w, conv_b, dtb, alog, dskip_e, nw, e_mat):
    L, W, GN = CHUNK, dm.W, dm.GN
    m = proj_a.shape[0]
    nc = m // nb // L
    rowblk = lambda b, c: b * nc + c
    par = lambda b, c: (layer, 0, 0)
    in_specs = [
        pl.BlockSpec((L, W), lambda b, c: (rowblk(b, c), 0)),
        pl.BlockSpec((L, W), lambda b, c: (rowblk(b, c), 1)),
        pl.BlockSpec((L, GN), lambda b, c: (rowblk(b, c), 2 * W // GN)),
        pl.BlockSpec((L, GN), lambda b, c: (rowblk(b, c), 2 * W // GN + 1)),
        pl.BlockSpec((L, dm.D), lambda b, c: (rowblk(b, c), 0)),
        pl.BlockSpec((dm.D, LANES), lambda b, c: (0, 0)),
        pl.BlockSpec((None, CONV_K, dm.CD), par),
        pl.BlockSpec((None, 1, dm.CD), par),
        pl.BlockSpec((None, 1, LANES), par),
        pl.BlockSpec((None, 1, LANES), par),
        pl.BlockSpec((None, 1, W), par),
        pl.BlockSpec((None, 1, W), par),
        pl.BlockSpec((LANES, W), lambda b, c: (0, 0)),
    ]
    out_specs = [
        pl.BlockSpec((L, W), lambda b, c: (rowblk(b, c), 0)),
        pl.BlockSpec((1, CONV_K - 1, dm.CD), lambda b, c: (b, 0, 0)),
        pl.BlockSpec((1, W, dm.N), lambda b, c: (b, 0, 0)),
    ]
    out_shape = [
        jax.ShapeDtypeStruct((m, W), _BF16),
        jax.ShapeDtypeStruct((nb, CONV_K - 1, dm.CD), _F32),
        jax.ShapeDtypeStruct((nb, W, dm.N), _F32),
    ]
    return pl.pallas_call(
        functools.partial(_ssd_prompt_kernel, dm),
        grid=(nb, nc),
        in_specs=in_specs,
        out_specs=out_specs,
        out_shape=out_shape,
        scratch_shapes=[pltpu.VMEM((SUBLANES, dm.CD), _F32),
                        pltpu.VMEM((L, dm.CD), _F32),
                        pltpu.VMEM((dm.N, W), _F32)],
        compiler_params=_cparams("arbitrary", "arbitrary"),
        name="ssd_prompt",
    )(proj_a, proj_a, proj_a, proj_a, h, w_dt, conv_w, conv_b, dtb, alog, dskip_e, nw, e_mat)


def _ret_log_gammas(n):
    return [float(np.log1p(-np.exp2(np.float32(-5.0 - h)))) for h in range(n)]


def _rotate(x, cos2, sin2):
    return x * cos2 + pltpu.roll(x, LANES // 2, axis=1) * sin2


def _ret_prompt_kernel(dm, q_ref, k_ref, v_ref, g_ref, cos_ref, sin_ref, nw_ref,
                       y_ref, st_out_ref, dec_ref, ec_ref, te_ref, st_ref):
    L, DK, DV = CHUNK, dm.DK, dm.DV
    b = pl.program_id(0)
    c = pl.program_id(1)
    nc = pl.num_programs(1)
    lgs = _ret_log_gammas(dm.RH)

    @pl.when(jnp.logical_and(b == 0, c == 0))
    def _():
        row = lax.broadcasted_iota(jnp.int32, (L, L), 0)
        col = lax.broadcasted_iota(jnp.int32, (L, L), 1)
        diff = (row - col).astype(_F32)
        rowf = row.astype(_F32)
        for h in range(dm.RH):
            dec_ref[h] = jnp.exp(jnp.where(row >= col, diff * lgs[h], -jnp.inf))
            ec_ref[h] = jnp.exp((rowf + 1.0) * lgs[h])
            te_ref[h] = jnp.exp((float(L - 1) - rowf) * lgs[h])

    @pl.when(c == 0)
    def _():
        st_ref[...] = jnp.zeros_like(st_ref)

    cos2 = cos_ref[...]
    sin2 = sin_ref[...]
    scale = DK ** -0.5
    for h in range(dm.RH):
        qs = slice(h * DK, (h + 1) * DK)
        vs = slice(h * DV, (h + 1) * DV)
        qr = _rotate(q_ref[:, qs], cos2, sin2)
        kr = _rotate(k_ref[:, qs], cos2, sin2) * scale
        q_bf = qr.astype(_BF16)
        k_bf = kr.astype(_BF16)
        v_h = v_ref[:, vs]
        scores = lax.dot_general(q_bf, k_bf, _NT, preferred_element_type=_F32)
        m_h = (scores * dec_ref[h]).astype(_BF16)
        st_h = st_ref[h]
        ec = ec_ref[h]
        te = te_ref[h]
        y = jnp.dot(m_h, v_h.astype(_BF16), preferred_element_type=_F32)
        y_inter = jnp.dot(q_bf, st_h.astype(_BF16), preferred_element_type=_F32)
        y = y + y_inter * jnp.concatenate([ec, ec], axis=1)
        vt = (v_h * jnp.concatenate([te, te], axis=1)).astype(_BF16)
        cs = lax.dot_general(k_bf, vt, _TN, preferred_element_type=_F32)
        st_ref[h] = st_h * ec[L - 1:L, 0:1] + cs
        ms = jnp.mean(y * y, axis=-1, keepdims=True)
        y = y * lax.rsqrt(ms + EPS) * nw_ref[:, vs] * _silu(g_ref[:, vs])
        y_ref[:, vs] = y.astype(y_ref.dtype)

    @pl.when(c == nc - 1)
    def _():
        for h in range(dm.RH):
            st_out_ref[0, h * DV:(h + 1) * DV, :] = st_ref[h].T


def _ret_prompt(dm, layer, proj_b, nb, cos2, sin2, nw):
    L, W, QK = CHUNK, dm.W, dm.QK
    m = proj_b.shape[0]
    nc = m // nb // L
    rowblk = lambda b, c: b * nc + c
    in_specs = [
        pl.BlockSpec((L, QK), lambda b, c: (rowblk(b, c), 0)),
        pl.BlockSpec((L, QK), lambda b, c: (rowblk(b, c), 1)),
        pl.BlockSpec((L, W), lambda b, c: (rowblk(b, c), 1)),
        pl.BlockSpec((L, W), lambda b, c: (rowblk(b, c), 2)),
        pl.BlockSpec((L, LANES), lambda b, c: (c, 0)),
        pl.BlockSpec((L, LANES), lambda b, c: (c, 0)),
        pl.BlockSpec((None, 1, W), lambda b, c: (layer, 0, 0)),
    ]
    out_specs = [
        pl.BlockSpec((L, W), lambda b, c: (rowblk(b, c), 0)),
        pl.BlockSpec((1, dm.RH * dm.DV, dm.DK), lambda b, c: (b, 0, 0)),
    ]
    out_shape = [
        jax.ShapeDtypeStruct((m, W), _BF16),
        jax.ShapeDtypeStruct((nb, dm.RH * dm.DV, dm.DK), _F32),
    ]
    return pl.pallas_call(
        functools.partial(_ret_prompt_kernel, dm),
        grid=(nb, nc),
        in_specs=in_specs,
        out_specs=out_specs,
        out_shape=out_shape,
        scratch_shapes=[pltpu.VMEM((dm.RH, L, L), _F32),
                        pltpu.VMEM((dm.RH, L, LANES), _F32),
                        pltpu.VMEM((dm.RH, L, LANES), _F32),
                        pltpu.VMEM((dm.RH, dm.DK, dm.DV), _F32)],
        compiler_params=_cparams("arbitrary", "arbitrary"),
        name="ret_prompt",
    )(proj_b, proj_b, proj_b, proj_b, cos2, sin2, nw)


def _block_mask(rows, width, block):
    lane = lax.broadcasted_iota(jnp.int32, (rows, width), 1)
    lo = lax.broadcasted_iota(jnp.int32, (rows, width), 0) * block
    return jnp.logical_and(lane >= lo, lane < lo + block)


def _own_block(mask, x):
    return jnp.sum(jnp.where(mask, x, 0.0), axis=0, keepdims=True)


def _pad_rows(x, rows):
    if x.shape[0] == rows:
        return x
    return jnp.concatenate([x, jnp.zeros((rows - x.shape[0], x.shape[1]), x.dtype)], axis=0)


def _ssd_step_one(dm, i, z_ref, xs_ref, b_ref, c_ref, dt_ref, cst_ref, st_ref, cw_ref, cb_ref,
                  dtb_ref, alog_ref, dskip_ref, nw_ref, e_ref, y_ref, conv_ref, st_out_ref):
    W, GN, GW, N, G = dm.W, dm.GN, dm.GW, dm.N, dm.G
    xbc = jnp.concatenate([xs_ref[i], b_ref[i], c_ref[i]], axis=1)
    cst = cst_ref[i]
    acc = cb_ref[...]
    for j in range(CONV_K - 1):
        acc = acc + cw_ref[j:j + 1, :] * cst[j:j + 1, :]
    acc = acc + cw_ref[CONV_K - 1:CONV_K, :] * xbc
    xc = _silu(acc)
    conv_ref[i] = jnp.concatenate([cst[1:CONV_K - 1, :], xbc], axis=0)

    dt = _softplus(dt_ref[i] + dtb_ref[...])
    decay = jnp.exp(dt * (-jnp.exp(alog_ref[...])))
    r8 = lax.broadcasted_iota(jnp.int32, (SUBLANES, LANES), 0)
    stacked = jnp.where(r8 == 0, jnp.broadcast_to(dt, (SUBLANES, LANES)),
                        jnp.where(r8 == 1, jnp.broadcast_to(decay, (SUBLANES, LANES)), 0.0))
    s_hi = stacked.astype(_BF16).astype(_F32)
    s_hl = jnp.concatenate([s_hi, stacked - s_hi], axis=0).astype(_BF16)
    ex = jnp.dot(s_hl, e_ref[...], preferred_element_type=_F32)
    ex = ex[0:SUBLANES] + ex[SUBLANES:2 * SUBLANES]
    dt_e, dec_e = ex[0:1], ex[1:2]
    xs_c = xc[:, 0:W]
    xdt = xs_c * dt_e
    b8 = jnp.concatenate([xc[:, W + g * N:W + (g + 1) * N] for g in range(G)], axis=0)
    c8 = jnp.concatenate([xc[:, W + GN + g * N:W + GN + (g + 1) * N] for g in range(G)], axis=0)
    gmask = _block_mask(G, W, GW)

    s_old = st_ref[i]
    y_all = lax.dot_general(c8.astype(_BF16), s_old.astype(_BF16), _NT,
                            preferred_element_type=_F32)
    y_old = _own_block(gmask, y_all)
    cb_e = _own_block(gmask, jnp.sum(c8 * b8, axis=-1, keepdims=True))
    y = dec_e * y_old + cb_e * xdt

    d_hi = dec_e.astype(_BF16).astype(_F32)
    d_lo = dec_e - d_hi
    kp = -(-3 * G // 16) * 16
    lhs = jnp.concatenate([jnp.where(gmask, d_hi, 0.0), jnp.where(gmask, d_lo, 0.0),
                           jnp.where(gmask, xdt, 0.0)], axis=0)
    lhs = _pad_rows(lhs, kp).astype(_BF16)
    ones_zeros = jnp.concatenate([jnp.ones((2 * G, N), _F32), jnp.zeros((2 * G, N), _F32)], axis=1)
    rhs = jnp.concatenate([ones_zeros, jnp.concatenate([jnp.zeros((G, N), _F32), b8], axis=1)], axis=0)
    rhs = _pad_rows(rhs, kp).astype(_BF16)
    res = lax.dot_general(lhs, rhs, _TN, preferred_element_type=_F32)
    st_out_ref[i] = s_old * res[:, 0:N] + res[:, N:2 * N]

    y = y + dskip_ref[...] * xs_c
    y = y * _silu(z_ref[i])
    ms = _own_block(gmask, jnp.sum(jnp.where(gmask, y * y, 0.0), axis=-1, keepdims=True) * (1.0 / GW))
    y_ref[i] = (y * lax.rsqrt(ms + EPS) * nw_ref[...]).astype(y_ref.dtype)


def _ssd_step_kernel(dm, bb, *refs):
    refs = refs[:14] + refs[-3:]
    for i in range(bb):
        _ssd_step_one(dm, i, *refs)


def _ssd_step(dm, layer, bb, proj3, proj3_dt, state_conv, state_ssd, prev_out, conv_w, conv_b, dtb, alog,
              dskip_e, nw, e_mat):
    W, GN = dm.W, dm.GN
    nb = proj3.shape[0]
    par = lambda b: (layer, 0, 0)
    in_specs = [
        pl.BlockSpec((bb, 1, W), lambda b: (b, 0, 0)),
        pl.BlockSpec((bb, 1, W), lambda b: (b, 0, 1)),
        pl.BlockSpec((bb, 1, GN), lambda b: (b, 0, 2 * W // GN)),
        pl.BlockSpec((bb, 1, GN), lambda b: (b, 0, 2 * W // GN + 1)),
        pl.BlockSpec((bb, 1, LANES), lambda b: (b, 0, 0)),
        pl.BlockSpec((None, bb, CONV_K - 1, dm.CD), lambda b: (layer, b, 0, 0)),
        pl.BlockSpec((None, bb, W, dm.N), lambda b: (layer, b, 0, 0)),
        pl.BlockSpec((None, CONV_K, dm.CD), par),
        pl.BlockSpec((None, 1, dm.CD), par),
        pl.BlockSpec((None, 1, LANES), par),
        pl.BlockSpec((None, 1, LANES), par),
        pl.BlockSpec((None, 1, W), par),
        pl.BlockSpec((None, 1, W), par),
        pl.BlockSpec((LANES, W), lambda b: (0, 0)),
    ]
    args = [proj3, proj3, proj3, proj3, proj3_dt, state_conv, state_ssd, conv_w, conv_b, dtb, alog,
            dskip_e, nw, e_mat]
    aliases = {}
    if prev_out is not None:
        in_specs.append(pl.BlockSpec(memory_space=pl.ANY))
        args.append(prev_out)
        aliases = {len(args) - 1: 2}
    out_specs = [
        pl.BlockSpec((bb, 1, W), lambda b: (b, 0, 0)),
        pl.BlockSpec((bb, CONV_K - 1, dm.CD), lambda b: (b, 0, 0)),
        pl.BlockSpec((None, bb, W, dm.N), lambda b: (layer, b, 0, 0)),
    ]
    out_shape = [
        jax.ShapeDtypeStruct((nb, 1, W), _BF16),
        jax.ShapeDtypeStruct((nb, CONV_K - 1, dm.CD), _F32),
        jax.ShapeDtypeStruct(state_ssd.shape, _F32),
    ]
    return pl.pallas_call(
        functools.partial(_ssd_step_kernel, dm, bb),
        grid=(nb // bb,),
        in_specs=in_specs,
        out_specs=out_specs,
        out_shape=out_shape,
        input_output_aliases=aliases,
        compiler_params=_cparams("arbitrary"),
        name="ssd_step",
    )(*args)


def _ret_step_one(dm, i, q_ref, k_ref, v_ref, g_ref, cos_ref, sin_ref, gam_ref, st_ref, nw_ref,
                  y_ref, st_out_ref):
    DK, DV, RH, W, QK = dm.DK, dm.DV, dm.RH, dm.W, dm.QK
    gammas = [float(np.exp(np.float32(lg))) for lg in _ret_log_gammas(RH)]
    cos_t, sin_t = cos_ref[...], sin_ref[...]
    lane = lax.broadcasted_iota(jnp.int32, (1, QK), 1)
    first_half = jnp.bitwise_and(lane, DK - 1) < DK // 2

    def rotate_heads(x):
        partner = jnp.where(first_half, pltpu.roll(x, QK - DK // 2, axis=1), pltpu.roll(x, DK // 2, axis=1))
        return x * cos_t + partner * sin_t

    qr = rotate_heads(q_ref[i])
    kr = rotate_heads(k_ref[i]) * (DK ** -0.5)
    q16 = jnp.concatenate([qr[:, h * DK:(h + 1) * DK] for h in range(RH)], axis=0)
    k16 = jnp.concatenate([kr[:, h * DK:(h + 1) * DK] for h in range(RH)], axis=0)
    v = v_ref[i]
    hmask = _block_mask(RH, W, DV)
    kp = -(-RH // 16) * 16

    s_old = st_ref[i]
    y_all = lax.dot_general(_pad_rows(q16, kp).astype(_BF16), s_old.astype(_BF16), _NT,
                            preferred_element_type=_F32)[0:RH]
    y_old = _own_block(hmask, y_all)
    qk_e = _own_block(hmask, jnp.sum(q16 * k16, axis=-1, keepdims=True))
    y = gam_ref[...] * y_old + qk_e * v
    lhs = _pad_rows(jnp.where(hmask, v, 0.0), kp).astype(_BF16)
    outer = lax.dot_general(lhs, _pad_rows(k16, kp).astype(_BF16), _TN,
                            preferred_element_type=_F32)
    for h in range(RH):
        rows = slice(h * DV, (h + 1) * DV)
        st_out_ref[i, rows, :] = gammas[h] * st_ref[i, rows, :] + outer[rows, :]
    ms = _own_block(hmask, jnp.sum(jnp.where(hmask, y * y, 0.0), axis=-1, keepdims=True) * (1.0 / DV))
    y = y * lax.rsqrt(ms + EPS) * nw_ref[...] * _silu(g_ref[i])
    y_ref[i] = y.astype(y_ref.dtype)


def _ret_step_kernel(dm, bb, *refs):
    refs = refs[:9] + refs[-2:]
    for i in range(bb):
        _ret_step_one(dm, i, *refs)


def _ret_step(dm, layer, bb, proj3, state_ret, prev_out, cos_t, sin_t, gam_e, nw):
    W, QK = dm.W, dm.QK
    nb = proj3.shape[0]
    cst = lambda b: (0, 0)
    in_specs = [
        pl.BlockSpec((bb, 1, QK), lambda b: (b, 0, 0)),
        pl.BlockSpec((bb, 1, QK), lambda b: (b, 0, 1)),
        pl.BlockSpec((bb, 1, W), lambda b: (b, 0, 1)),
        pl.BlockSpec((bb, 1, W), lambda b: (b, 0, 2)),
        pl.BlockSpec((1, QK), cst),
        pl.BlockSpec((1, QK), cst),
        pl.BlockSpec((1, W), cst),
        pl.BlockSpec((None, bb, dm.RH * dm.DV, dm.DK), lambda b: (layer, b, 0, 0)),
        pl.BlockSpec((None, 1, W), lambda b: (layer, 0, 0)),
    ]
    args = [proj3, proj3, proj3, proj3, cos_t, sin_t, gam_e, state_ret, nw]
    aliases = {}
    if prev_out is not None:
        in_specs.append(pl.BlockSpec(memory_space=pl.ANY))
        args.append(prev_out)
        aliases = {len(args) - 1: 1}
    out_specs = [
        pl.BlockSpec((bb, 1, W), lambda b: (b, 0, 0)),
        pl.BlockSpec((None, bb, dm.RH * dm.DV, dm.DK), lambda b: (layer, b, 0, 0)),
    ]
    out_shape = [
        jax.ShapeDtypeStruct((nb, 1, W), _BF16),
        jax.ShapeDtypeStruct(state_ret.shape, _F32),
    ]
    return pl.pallas_call(
        functools.partial(_ret_step_kernel, dm, bb),
        grid=(nb // bb,),
        in_specs=in_specs,
        out_specs=out_specs,
        out_shape=out_shape,
        input_output_aliases=aliases,
        compiler_params=_cparams("arbitrary"),
        name="ret_step",
    )(*args)


def _rope_tables(dk, positions):
    half = dk // 2
    inv = 1.0 / (ROPE_BASE ** (jnp.arange(half, dtype=_F32) / half))
    ang = positions.astype(_F32)[:, None] * inv[None, :]
    cos = jnp.cos(ang)
    sin = jnp.sin(ang)
    return jnp.concatenate([cos, cos], axis=1), jnp.concatenate([-sin, sin], axis=1)


def _pad_lanes(v):
    return jnp.pad(v, ((0, 0), (0, LANES - v.shape[1])))[:, None, :]


def kernel(x_prompt, x_sample, state_conv, state_ssd, state_ret, norm_w, w_in, conv_w, conv_b, dt_bias,
           a_log, d_skip, ssd_norm_w, ret_norm_w, w_out, final_norm_w):
    bp, tp, d_model = x_prompt.shape
    bs, ts, _ = x_sample.shape
    depth = norm_w.shape[0]
    assert ts == 1 and tp % CHUNK == 0
    dm = Dims(d_model, dt_bias.shape[1], conv_w.shape[2], state_ret.shape[2], state_ret.shape[3],
              state_ret.shape[4])
    assert w_in.shape[2] == dm.IN_COLS
    W = dm.W

    w_in_t = jnp.swapaxes(w_in, 1, 2)
    off_b = dm.NA + dm.H
    w_a = _transpose_cast_weights(w_in_t, 0, 0, dm.NA, None, 256)
    w_b = _transpose_cast_weights(w_in_t, 0, off_b, dm.NB, None, 256)
    w_dt = [_transpose_cast_weights(w_in_t, l, dm.NA, LANES, off_b, LANES) for l in range(depth)]
    w_o0 = _cast_weights(w_out, 0, min(1024, 2 * d_model), 512)
    w_o1, w_o2 = (w_o0, 0), (w_o0, 1)

    heads = jnp.arange(LANES)[:, None]
    lanes = jnp.arange(W)[None, :]
    e_mat = (lanes // dm.P == heads).astype(_BF16)
    cos_p, sin_p = _rope_tables(dm.DK, jnp.arange(tp))
    cos_s, sin_s = _rope_tables(dm.DK, PAST_LEN + jnp.arange(ts))
    cos_s, sin_s = jnp.tile(cos_s, (1, dm.RH)), jnp.tile(sin_s, (1, dm.RH))
    gam_e = jnp.repeat(jnp.exp(jnp.asarray(_ret_log_gammas(dm.RH), _F32)), dm.DV)[None, :]
    cb3 = conv_b[:, None, :]
    dtb3, alog3 = _pad_lanes(dt_bias), _pad_lanes(a_log)
    dskip3 = jnp.repeat(d_skip, dm.P, axis=1)[:, None, :]
    snw3, rnw3 = ssd_norm_w[:, None, :], ret_norm_w[:, None, :]

    st_ssd_in = state_ssd.reshape(depth, bs, dm.H * dm.P, dm.N)
    st_ret_in = state_ret.reshape(depth, bs, dm.RH * dm.DV, dm.DK)

    xp = x_prompt.reshape(bp * tp, d_model)
    xs = x_sample.reshape(bs * ts, d_model)
    mp = xp.shape[0]
    tm_p = min(1024, mp)
    tn_o = _largest_divisor((512, 256), d_model)
    tm_n = min(256, mp)
    tn_a = _largest_divisor((1024, 512), dm.NA)
    tn_b = _largest_divisor((1024, 512), dm.NB)
    tn_s = _largest_divisor((2048, 1024, 512), dm.NA, dm.NB)
    tn_so = _largest_divisor((1024, 512), d_model)
    bb = _largest_divisor((4, 2, 1), bs)

    conv_p, ssd_p, ret_p, conv_s = [], [], [], []
    ssd_s = None
    ret_s = None
    for l in range(depth):
        more = l + 1 < depth
        h = _rmsnorm(xp, norm_w[l], _BF16, tm_n)
        proj_a, w_a_next = _proj(h, w_a, tm_p, tn_a, ("transposed", w_in_t, l + 1, 0, dm.NA) if more else None)
        proj_b, w_b_next = _proj(h, w_b, tm_p, tn_b, ("transposed", w_in_t, l + 1, off_b, dm.NB) if more else None)
        y_ssd, c_new, s_new = _ssd_prompt(dm, l, proj_a, h, w_dt[l], bp, conv_w, cb3, dtb3, alog3, dskip3, snw3,
                                          e_mat)
        y_ret, r_new = _ret_prompt(dm, l, proj_b, bp, cos_p, sin_p, rnw3)
        xp, w_o1_next = _proj(y_ssd, w_o1[0], tm_p, tn_o, ("plain", w_out, l + 1, 0, W) if more else None,
                              resid=xp, krow=w_o1[1])
        xp, w_o2_next = _proj(y_ret, w_o2[0], tm_p, tn_o, ("plain", w_out, l + 1, W, W) if more else None,
                              resid=xp, krow=w_o2[1])
        conv_p.append(c_new)
        ssd_p.append(s_new.reshape(bp, dm.H, dm.P, dm.N))
        ret_p.append(r_new.reshape(bp, dm.RH, dm.DV, dm.DK))

        h = _rmsnorm(xs, norm_w[l], _BF16, bs)
        proj3_a = _proj(h, w_a, bs, tn_s)[0].reshape(bs, 1, dm.NA)
        proj3_b = _proj(h, w_b, bs, tn_s)[0].reshape(bs, 1, dm.NB)
        proj3_dt = _proj(h, w_dt[l], bs, LANES)[0].reshape(bs, 1, LANES)
        y_ssd, c_new, ssd_s = _ssd_step(dm, l, bb, proj3_a, proj3_dt, state_conv, st_ssd_in, ssd_s, conv_w, cb3,
                                        dtb3, alog3, dskip3, snw3, e_mat)
        y_ret, ret_s = _ret_step(dm, l, bb, proj3_b, st_ret_in, ret_s, cos_s, sin_s, gam_e, rnw3)
        xs = _outproj(y_ssd.reshape(bs, W), y_ret.reshape(bs, W), w_o1, w_o2, xs, bs, tn_so)
        conv_s.append(c_new)
        w_a, w_b, w_o1, w_o2 = w_a_next, w_b_next, (w_o1_next, 0), (w_o2_next, 0)

    y_prompt = _rmsnorm(xp, final_norm_w, _F32, tm_n).reshape(bp, tp, d_model)
    y_sample = _rmsnorm(xs, final_norm_w, _F32, bs).reshape(bs, ts, d_model)
    return (y_prompt, y_sample, jnp.stack(conv_p), jnp.stack(ssd_p), jnp.stack(ret_p),
            jnp.stack(conv_s),
            ssd_s.reshape(depth, bs, dm.H, dm.P, dm.N),
            ret_s.reshape(depth, bs, dm.RH, dm.DV, dm.DK))
```

```python
import functools

import numpy as np
import jax
import jax.numpy as jnp
from jax import lax
from jax.experimental import pallas as pl
from jax.experimental.pallas import tpu as pltpu

SSD_HEADDIM = 64
SSD_STATE = 128
CONV_K = 4
CHUNK = 128
ROPE_BASE = 10000.0
EPS = 1e-5
PAST_LEN = 16384

LANES = 128
SUBLANES = 8
VMEM_LIMIT_BYTES = 56 * 1024 * 1024

_F32 = jnp.float32
_BF16 = jnp.bfloat16
_NT = (((1,), (1,)), ((), ()))
_TN = (((0,), (0,)), ((), ()))


def _cparams(*sem):
    return pltpu.CompilerParams(dimension_semantics=sem, vmem_limit_bytes=VMEM_LIMIT_BYTES)


def _silu(x):
    h = 0.5 * x
    return h + h * jnp.tanh(h)


def _softplus(x):
    return jnp.maximum(x, 0.0) + jnp.log1p(jnp.exp(-jnp.abs(x)))


def _split2(x):
    hi = x.astype(_BF16)
    lo = (x - hi.astype(_F32)).astype(_BF16)
    return hi, lo


def _split3(x):
    hi = x.astype(_BF16)
    r = x - hi.astype(_F32)
    mid = r.astype(_BF16)
    lo = (r - mid.astype(_F32)).astype(_BF16)
    return hi, mid, lo


def _largest_divisor(candidates, *values):
    for c in candidates:
        if all(v % c == 0 for v in values):
            return c
    raise ValueError((candidates, values))


class Dims:
    def __init__(self, d_model, n_heads, conv_dim, ret_heads, ret_dv, ret_dk):
        self.D = d_model
        self.W = d_model
        self.H = n_heads
        self.P = SSD_HEADDIM
        self.N = SSD_STATE
        self.GN = (conv_dim - self.W) // 2
        self.G = self.GN // self.N
        self.HPG = self.H // self.G
        self.GW = self.HPG * self.P
        self.CD = conv_dim
        self.RH = ret_heads
        self.DV = ret_dv
        self.DK = ret_dk
        self.QK = ret_heads * ret_dk
        self.NA = 2 * self.W + 2 * self.GN
        self.NB = 2 * self.QK + 2 * self.W
        self.IN_COLS = self.NA + self.H + self.NB
        assert self.H < LANES and self.P * 2 == LANES and self.N == LANES
        assert self.DK == LANES and self.DV == 2 * LANES and 2 * self.QK == self.W
        assert self.GW % LANES == 0 and self.W % self.GW == 0
        assert (2 * self.W) % self.GN == 0


def _cast_kernel(w_ref, o_ref):
    o_ref[...] = w_ref[...].astype(o_ref.dtype)


def _cast_weights(w, layer, tk, tn):
    _, k, n = w.shape
    return pl.pallas_call(
        _cast_kernel,
        grid=(k // tk, n // tn),
        in_specs=[pl.BlockSpec((None, tk, tn), lambda i, j: (layer, i, j))],
        out_specs=pl.BlockSpec((tk, tn), lambda i, j: (i, j)),
        out_shape=jax.ShapeDtypeStruct((k, n), _BF16),
        compiler_params=_cparams("parallel", "parallel"),
        name="cast_weights",
    )(w)


def _transpose_cast_kernel(row_off, limit, wt_ref, o_ref):
    x = wt_ref[0]
    if limit is not None:
        row = (row_off + pl.program_id(1) * x.shape[0]
               + lax.broadcasted_iota(jnp.int32, x.shape, 0))
        x = jnp.where(row < limit, x, 0.0)
    o_ref[...] = x.T.astype(o_ref.dtype)


def _transposed_rows_spec(layer, rows, k, row_index):
    return pl.BlockSpec((pl.Element(1), pl.Element(rows), pl.Element(k)),
                        lambda *ids: (layer, pl.multiple_of(row_index(*ids), SUBLANES), 0))


def _transpose_cast_weights(wt, layer, row_off, n_out, limit, tn):
    k = wt.shape[2]
    return pl.pallas_call(
        functools.partial(_transpose_cast_kernel, row_off, limit),
        grid=(1, n_out // tn),
        in_specs=[_transposed_rows_spec(layer, tn, k, lambda i, j: row_off + j * tn)],
        out_specs=pl.BlockSpec((k, tn), lambda i, j: (0, j)),
        out_shape=jax.ShapeDtypeStruct((k, n_out), _BF16),
        compiler_params=_cparams("parallel", "parallel"),
        name="transpose_cast_weights",
    )(wt)


def _norm_kernel(x_ref, w_ref, o_ref):
    x = x_ref[...]
    ms = jnp.mean(x * x, axis=-1, keepdims=True)
    o_ref[...] = (x * lax.rsqrt(ms + EPS) * w_ref[...]).astype(o_ref.dtype)


def _rmsnorm(x, w, out_dtype, tm):
    m, d = x.shape
    return pl.pallas_call(
        _norm_kernel,
        grid=(m // tm,),
        in_specs=[pl.BlockSpec((tm, d), lambda i: (i, 0)),
                  pl.BlockSpec((1, d), lambda i: (0, 0))],
        out_specs=pl.BlockSpec((tm, d), lambda i: (i, 0)),
        out_shape=jax.ShapeDtypeStruct((m, d), out_dtype),
        compiler_params=_cparams("parallel"),
        name="rmsnorm",
    )(x, w.reshape(1, d))


def _inproj_kernel(has_nxt, a_ref, w_ref, *rest):
    if has_nxt:
        wn_ref, o_ref, wo_ref = rest
        wo_ref[...] = wn_ref[0].T.astype(wo_ref.dtype)
    else:
        (o_ref,) = rest
    o_ref[...] = jnp.dot(a_ref[...], w_ref[...], preferred_element_type=_F32)


def _inproj(a, w, tm, tn, nxt=None):
    m, k = a.shape
    n = w.shape[1]
    ni, nj = m // tm, n // tn
    in_specs = [pl.BlockSpec((tm, k), lambda i, j: (i, 0)),
                pl.BlockSpec((k, tn), lambda i, j: (0, j))]
    out_specs = [pl.BlockSpec((tm, tn), lambda i, j: (i, j))]
    out_shape = [jax.ShapeDtypeStruct((m, n), _F32)]
    args = [a, w]
    if nxt is not None:
        wt, layer, row_off, n_out = nxt
        rows = n_out // (ni * nj)
        assert rows * ni * nj == n_out and rows % LANES == 0
        in_specs.append(_transposed_rows_spec(layer, rows, k, lambda i, j: row_off + (i * nj + j) * rows))
        out_specs.append(pl.BlockSpec((k, rows), lambda i, j: (0, i * nj + j)))
        out_shape.append(jax.ShapeDtypeStruct((k, n_out), _BF16))
        args.append(wt)
    res = pl.pallas_call(
        functools.partial(_inproj_kernel, nxt is not None),
        grid=(ni, nj),
        in_specs=in_specs,
        out_specs=out_specs,
        out_shape=out_shape,
        compiler_params=_cparams("arbitrary", "arbitrary"),
        name="inproj",
    )(*args)
    return res if nxt is not None else (res[0], None)


def _outproj_kernel(has_nxt, a1_ref, a2_ref, w1_ref, w2_ref, x_ref, *rest):
    if has_nxt:
        wn_ref, o_ref, wo_ref = rest
        wo_ref[...] = wn_ref[...].astype(wo_ref.dtype)
    else:
        (o_ref,) = rest
    acc = jnp.dot(a1_ref[...], w1_ref[...], preferred_element_type=_F32)
    acc = acc + jnp.dot(a2_ref[...], w2_ref[...], preferred_element_type=_F32)
    o_ref[...] = x_ref[...] + acc


def _outproj(a1, a2, w, x, tm, tn, nxt=None):
    m, k = a1.shape
    n = w.shape[1]
    ni, nj = m // tm, n // tn
    in_specs = [pl.BlockSpec((tm, k), lambda i, j: (i, 0)),
                pl.BlockSpec((tm, k), lambda i, j: (i, 0)),
                pl.BlockSpec((k, tn), lambda i, j: (0, j)),
                pl.BlockSpec((k, tn), lambda i, j: (1, j)),
                pl.BlockSpec((tm, tn), lambda i, j: (i, j))]
    out_specs = [pl.BlockSpec((tm, tn), lambda i, j: (i, j))]
    out_shape = [jax.ShapeDtypeStruct((m, n), _F32)]
    args = [a1, a2, w, w, x]
    if nxt is not None:
        wf, layer = nxt
        rows = 2 * k // (ni * nj)
        assert rows * ni * nj == 2 * k and rows % (2 * SUBLANES) == 0
        in_specs.append(pl.BlockSpec((None, rows, n), lambda i, j: (layer, i * nj + j, 0)))
        out_specs.append(pl.BlockSpec((rows, n), lambda i, j: (i * nj + j, 0)))
        out_shape.append(jax.ShapeDtypeStruct((2 * k, n), _BF16))
        args.append(wf)
    res = pl.pallas_call(
        functools.partial(_outproj_kernel, nxt is not None),
        grid=(ni, nj),
        in_specs=in_specs,
        out_specs=out_specs,
        out_shape=out_shape,
        compiler_params=_cparams("arbitrary", "arbitrary"),
        name="outproj",
    )(*args)
    return res if nxt is not None else (res[0], None)


def _conv_slabs(dm, in_refs, tail_ref, cw_ref, cb_ref, xc_ref):
    L = CHUNK
    col0 = 0
    for ref in in_refs:
        width = ref.shape[1]
        slab = _largest_divisor((512, 256, LANES), width)
        rowi = lax.broadcasted_iota(jnp.int32, (SUBLANES, slab), 0)
        for s in range(width // slab):
            src = slice(s * slab, (s + 1) * slab)
            dst = slice(col0 + s * slab, col0 + (s + 1) * slab)
            wts = [jnp.broadcast_to(cw_ref[j:j + 1, dst], (SUBLANES, slab)) for j in range(CONV_K)]
            bias = jnp.broadcast_to(cb_ref[:, dst], (SUBLANES, slab))
            prev = tail_ref[:, dst]
            prev_rolled = [pltpu.roll(prev, sh, axis=0) for sh in range(1, CONV_K)]
            for i in range(L // SUBLANES):
                cur = ref[i * SUBLANES:(i + 1) * SUBLANES, src]
                cur_rolled = [pltpu.roll(cur, sh, axis=0) for sh in range(1, CONV_K)]
                acc = bias
                for j in range(CONV_K - 1):
                    sh = CONV_K - 1 - j
                    tap = jnp.where(rowi >= sh, cur_rolled[sh - 1], prev_rolled[sh - 1])
                    acc = acc + wts[j] * tap
                acc = acc + wts[CONV_K - 1] * cur
                xc_ref[i * SUBLANES:(i + 1) * SUBLANES, dst] = _silu(acc)
                prev_rolled = cur_rolled
            tail_ref[:, dst] = ref[L - SUBLANES:L, src]
        col0 += width


def _ssd_prompt_kernel(dm, z_ref, xs_ref, b_ref, c_ref, dt_ref, cw_ref, cb_ref, dtb_ref, alog_ref,
                       dskip_ref, nw_ref, e_ref, y_ref, conv_ref, st_out_ref, tail_ref, xc_ref, st_ref):
    L, W, GN, GW, N = CHUNK, dm.W, dm.GN, dm.GW, dm.N
    c = pl.program_id(1)
    nc = pl.num_programs(1)

    @pl.when(c == 0)
    def _():
        tail_ref[...] = jnp.zeros_like(tail_ref)
        st_ref[...] = jnp.zeros_like(st_ref)

    _conv_slabs(dm, (xs_ref, b_ref, c_ref), tail_ref, cw_ref, cb_ref, xc_ref)
    keep = CONV_K - 1
    conv_ref[0, :, 0:W] = xs_ref[L - keep:L, :]
    conv_ref[0, :, W:W + GN] = b_ref[L - keep:L, :]
    conv_ref[0, :, W + GN:W + 2 * GN] = c_ref[L - keep:L, :]

    dt = _softplus(dt_ref[...] + dtb_ref[...])
    a_neg = -jnp.exp(alog_ref[...])
    la = dt * a_neg
    row = lax.broadcasted_iota(jnp.int32, (L, L), 0)
    col = lax.broadcasted_iota(jnp.int32, (L, L), 1)
    causal = row >= col
    upper = (row <= col).astype(_BF16)
    cum_t = None
    for part in _split3(la.T):
        d = jnp.dot(part, upper, preferred_element_type=_F32)
        cum_t = d if cum_t is None else cum_t + d
    cum = cum_t.T
    ecum = jnp.exp(cum)
    to_end = jnp.exp(cum[L - 1:L, :] - cum)
    stacked = jnp.concatenate([dt, ecum, to_end], axis=0)
    s_hi, s_lo = _split2(stacked)
    lane = lax.broadcasted_iota(jnp.int32, (L, LANES), 1)
    low_half = lane < dm.P

    for g in range(dm.G):
        sl = slice(g * GW, (g + 1) * GW)
        e_g = e_ref[:, sl]
        ex = (jnp.dot(s_hi, e_g, preferred_element_type=_F32)
              + jnp.dot(s_lo, e_g, preferred_element_type=_F32))
        dt_e, ecum_e, toe_e = ex[0:L], ex[L:2 * L], ex[2 * L:3 * L]
        xs_g = xc_ref[:, sl]
        xdt = xs_g * dt_e
        b_g = xc_ref[:, W + g * N:W + (g + 1) * N]
        c_g = xc_ref[:, W + GN + g * N:W + GN + (g + 1) * N]
        b_bf = b_g.astype(_BF16)
        c_bf = c_g.astype(_BF16)
        scores = lax.dot_general(c_bf, b_bf, _NT, preferred_element_type=_F32)
        st_g = st_ref[:, sl]
        y_parts = []
        for j in range(GW // LANES):
            xpair = xdt[:, j * LANES:(j + 1) * LANES].astype(_BF16)
            acc = None
            for which in range(2):
                h = g * dm.HPG + 2 * j + which
                seg = cum[:, h:h + 1] - cum_t[h:h + 1, :]
                decay = jnp.exp(jnp.where(causal, seg, -jnp.inf))
                m_h = (scores * decay).astype(_BF16)
                keep_lanes = low_half if which == 0 else jnp.logical_not(low_half)
                x_h = jnp.where(keep_lanes, xpair, jnp.zeros_like(xpair))
                d = jnp.dot(m_h, x_h, preferred_element_type=_F32)
                acc = d if acc is None else acc + d
            y_parts.append(acc)
        y = jnp.concatenate(y_parts, axis=1)
        y = y + jnp.dot(c_bf, st_g.astype(_BF16), preferred_element_type=_F32) * ecum_e
        w_g = (xdt * toe_e).astype(_BF16)
        cs = lax.dot_general(b_bf, w_g, _TN, preferred_element_type=_F32)
        st_ref[:, sl] = st_g * ecum_e[L - 1:L, :] + cs
        y = y + dskip_ref[:, sl] * xs_g
        y = y * _silu(z_ref[:, sl])
        ms = jnp.mean(y * y, axis=-1, keepdims=True)
        y_ref[:, sl] = (y * lax.rsqrt(ms + EPS) * nw_ref[:, sl]).astype(y_ref.dtype)

    @pl.when(c == nc - 1)
    def _():
        for j in range(W // LANES):
            st_out_ref[0, j * LANES:(j + 1) * LANES, :] = st_ref[:, j * LANES:(j + 1) * LANES].T


def _ssd_prompt(dm, layer, proj_a, proj_dt, nb, conv_w, conv_b, dtb, alog, dskip_e, nw, e_mat):
    L, W, GN = CHUNK, dm.W, dm.GN
    m = proj_a.shape[0]
    nc = m // nb // L
    rowblk = lambda b, c: b * nc + c
    par = lambda b, c: (layer, 0, 0)
    in_specs = [
        pl.BlockSpec((L, W), lambda b, c: (rowblk(b, c), 0)),
        pl.BlockSpec((L, W), lambda b, c: (rowblk(b, c), 1)),
        pl.BlockSpec((L, GN), lambda b, c: (rowblk(b, c), 2 * W // GN)),
        pl.BlockSpec((L, GN), lambda b, c: (rowblk(b, c), 2 * W // GN + 1)),
        pl.BlockSpec((L, LANES), lambda b, c: (rowblk(b, c), 0)),
        pl.BlockSpec((None, CONV_K, dm.CD), par),
        pl.BlockSpec((None, 1, dm.CD), par),
        pl.BlockSpec((None, 1, LANES), par),
        pl.BlockSpec((None, 1, LANES), par),
        pl.BlockSpec((None, 1, W), par),
        pl.BlockSpec((None, 1, W), par),
        pl.BlockSpec((LANES, W), lambda b, c: (0, 0)),
    ]
    out_specs = [
        pl.BlockSpec((L, W), lambda b, c: (rowblk(b, c), 0)),
        pl.BlockSpec((1, CONV_K - 1, dm.CD), lambda b, c: (b, 0, 0)),
        pl.BlockSpec((1, W, dm.N), lambda b, c: (b, 0, 0)),
    ]
    out_shape = [
        jax.ShapeDtypeStruct((m, W), _BF16),
        jax.ShapeDtypeStruct((nb, CONV_K - 1, dm.CD), _F32),
        jax.ShapeDtypeStruct((nb, W, dm.N), _F32),
    ]
    return pl.pallas_call(
        functools.partial(_ssd_prompt_kernel, dm),
        grid=(nb, nc),
        in_specs=in_specs,
        out_specs=out_specs,
        out_shape=out_shape,
        scratch_shapes=[pltpu.VMEM((SUBLANES, dm.CD), _F32),
                        pltpu.VMEM((L, dm.CD), _F32),
                        pltpu.VMEM((dm.N, W), _F32)],
        compiler_params=_cparams("arbitrary", "arbitrary"),
        name="ssd_prompt",
    )(proj_a, proj_a, proj_a, proj_a, proj_dt, conv_w, conv_b, dtb, alog, dskip_e, nw, e_mat)


def _ret_log_gammas(n):
    return [float(np.log1p(-np.exp2(np.float32(-5.0 - h)))) for h in range(n)]


def _rotate(x, cos2, sin2):
    return x * cos2 + pltpu.roll(x, LANES // 2, axis=1) * sin2


def _ret_prompt_kernel(dm, q_ref, k_ref, v_ref, g_ref, cos_ref, sin_ref, nw_ref,
                       y_ref, st_out_ref, dec_ref, ec_ref, te_ref, st_ref):
    L, DK, DV = CHUNK, dm.DK, dm.DV
    b = pl.program_id(0)
    c = pl.program_id(1)
    nc = pl.num_programs(1)
    lgs = _ret_log_gammas(dm.RH)

    @pl.when(jnp.logical_and(b == 0, c == 0))
    def _():
        row = lax.broadcasted_iota(jnp.int32, (L, L), 0)
        col = lax.broadcasted_iota(jnp.int32, (L, L), 1)
        diff = (row - col).astype(_F32)
        rowf = row.astype(_F32)
        for h in range(dm.RH):
            dec_ref[h] = jnp.exp(jnp.where(row >= col, diff * lgs[h], -jnp.inf))
            ec_ref[h] = jnp.exp((rowf + 1.0) * lgs[h])
            te_ref[h] = jnp.exp((float(L - 1) - rowf) * lgs[h])

    @pl.when(c == 0)
    def _():
        st_ref[...] = jnp.zeros_like(st_ref)

    cos2 = cos_ref[...]
    sin2 = sin_ref[...]
    scale = DK ** -0.5
    for h in range(dm.RH):
        qs = slice(h * DK, (h + 1) * DK)
        vs = slice(h * DV, (h + 1) * DV)
        qr = _rotate(q_ref[:, qs], cos2, sin2)
        kr = _rotate(k_ref[:, qs], cos2, sin2) * scale
        q_bf = qr.astype(_BF16)
        k_bf = kr.astype(_BF16)
        v_h = v_ref[:, vs]
        scores = lax.dot_general(q_bf, k_bf, _NT, preferred_element_type=_F32)
        m_h = (scores * dec_ref[h]).astype(_BF16)
        st_h = st_ref[h]
        ec = ec_ref[h]
        te = te_ref[h]
        y = jnp.dot(m_h, v_h.astype(_BF16), preferred_element_type=_F32)
        y_inter = jnp.dot(q_bf, st_h.astype(_BF16), preferred_element_type=_F32)
        y = y + y_inter * jnp.concatenate([ec, ec], axis=1)
        vt = (v_h * jnp.concatenate([te, te], axis=1)).astype(_BF16)
        cs = lax.dot_general(k_bf, vt, _TN, preferred_element_type=_F32)
        st_ref[h] = st_h * ec[L - 1:L, 0:1] + cs
        ms = jnp.mean(y * y, axis=-1, keepdims=True)
        y = y * lax.rsqrt(ms + EPS) * nw_ref[:, vs] * _silu(g_ref[:, vs])
        y_ref[:, vs] = y.astype(y_ref.dtype)

    @pl.when(c == nc - 1)
    def _():
        for h in range(dm.RH):
            st_out_ref[0, h * DV:(h + 1) * DV, :] = st_ref[h].T


def _ret_prompt(dm, layer, proj_b, nb, cos2, sin2, nw):
    L, W, QK = CHUNK, dm.W, dm.QK
    m = proj_b.shape[0]
    nc = m // nb // L
    rowblk = lambda b, c: b * nc + c
    in_specs = [
        pl.BlockSpec((L, QK), lambda b, c: (rowblk(b, c), 0)),
        pl.BlockSpec((L, QK), lambda b, c: (rowblk(b, c), 1)),
        pl.BlockSpec((L, W), lambda b, c: (rowblk(b, c), 1)),
        pl.BlockSpec((L, W), lambda b, c: (rowblk(b, c), 2)),
        pl.BlockSpec((L, LANES), lambda b, c: (c, 0)),
        pl.BlockSpec((L, LANES), lambda b, c: (c, 0)),
        pl.BlockSpec((None, 1, W), lambda b, c: (layer, 0, 0)),
    ]
    out_specs = [
        pl.BlockSpec((L, W), lambda b, c: (rowblk(b, c), 0)),
        pl.BlockSpec((1, dm.RH * dm.DV, dm.DK), lambda b, c: (b, 0, 0)),
    ]
    out_shape = [
        jax.ShapeDtypeStruct((m, W), _BF16),
        jax.ShapeDtypeStruct((nb, dm.RH * dm.DV, dm.DK), _F32),
    ]
    return pl.pallas_call(
        functools.partial(_ret_prompt_kernel, dm),
        grid=(nb, nc),
        in_specs=in_specs,
        out_specs=out_specs,
        out_shape=out_shape,
        scratch_shapes=[pltpu.VMEM((dm.RH, L, L), _F32),
                        pltpu.VMEM((dm.RH, L, LANES), _F32),
                        pltpu.VMEM((dm.RH, L, LANES), _F32),
                        pltpu.VMEM((dm.RH, dm.DK, dm.DV), _F32)],
        compiler_params=_cparams("arbitrary", "arbitrary"),
        name="ret_prompt",
    )(proj_b, proj_b, proj_b, proj_b, cos2, sin2, nw)


def _block_mask(rows, width, block):
    lane = lax.broadcasted_iota(jnp.int32, (rows, width), 1)
    lo = lax.broadcasted_iota(jnp.int32, (rows, width), 0) * block
    return jnp.logical_and(lane >= lo, lane < lo + block)


def _own_block(mask, x):
    return jnp.sum(jnp.where(mask, x, 0.0), axis=0, keepdims=True)


def _pad_rows(x, rows):
    if x.shape[0] == rows:
        return x
    return jnp.concatenate([x, jnp.zeros((rows - x.shape[0], x.shape[1]), x.dtype)], axis=0)


def _ssd_step_one(dm, i, z_ref, xs_ref, b_ref, c_ref, dt_ref, cst_ref, st_ref, cw_ref, cb_ref,
                  dtb_ref, alog_ref, dskip_ref, nw_ref, e_ref, y_ref, conv_ref, st_out_ref):
    W, GN, GW, N, G = dm.W, dm.GN, dm.GW, dm.N, dm.G
    xbc = jnp.concatenate([xs_ref[i], b_ref[i], c_ref[i]], axis=1)
    cst = cst_ref[i]
    acc = cb_ref[...]
    for j in range(CONV_K - 1):
        acc = acc + cw_ref[j:j + 1, :] * cst[j:j + 1, :]
    acc = acc + cw_ref[CONV_K - 1:CONV_K, :] * xbc
    xc = _silu(acc)
    conv_ref[i] = jnp.concatenate([cst[1:CONV_K - 1, :], xbc], axis=0)

    dt = _softplus(dt_ref[i] + dtb_ref[...])
    decay = jnp.exp(dt * (-jnp.exp(alog_ref[...])))
    r8 = lax.broadcasted_iota(jnp.int32, (SUBLANES, LANES), 0)
    stacked = jnp.where(r8 == 0, jnp.broadcast_to(dt, (SUBLANES, LANES)),
                        jnp.where(r8 == 1, jnp.broadcast_to(decay, (SUBLANES, LANES)), 0.0))
    s_hi = stacked.astype(_BF16).astype(_F32)
    s_hl = jnp.concatenate([s_hi, stacked - s_hi], axis=0).astype(_BF16)
    ex = jnp.dot(s_hl, e_ref[...], preferred_element_type=_F32)
    ex = ex[0:SUBLANES] + ex[SUBLANES:2 * SUBLANES]
    dt_e, dec_e = ex[0:1], ex[1:2]
    xs_c = xc[:, 0:W]
    xdt = xs_c * dt_e
    b8 = jnp.concatenate([xc[:, W + g * N:W + (g + 1) * N] for g in range(G)], axis=0)
    c8 = jnp.concatenate([xc[:, W + GN + g * N:W + GN + (g + 1) * N] for g in range(G)], axis=0)
    gmask = _block_mask(G, W, GW)

    s_old = st_ref[i]
    y_all = lax.dot_general(c8.astype(_BF16), s_old.astype(_BF16), _NT,
                            preferred_element_type=_F32)
    y_old = _own_block(gmask, y_all)
    cb_e = _own_block(gmask, jnp.sum(c8 * b8, axis=-1, keepdims=True))
    y = dec_e * y_old + cb_e * xdt

    d_hi = dec_e.astype(_BF16).astype(_F32)
    d_lo = dec_e - d_hi
    kp = -(-3 * G // 16) * 16
    lhs = jnp.concatenate([jnp.where(gmask, d_hi, 0.0), jnp.where(gmask, d_lo, 0.0),
                           jnp.where(gmask, xdt, 0.0)], axis=0)
    lhs = _pad_rows(lhs, kp).astype(_BF16)
    ones_zeros = jnp.concatenate([jnp.ones((2 * G, N), _F32), jnp.zeros((2 * G, N), _F32)], axis=1)
    rhs = jnp.concatenate([ones_zeros, jnp.concatenate([jnp.zeros((G, N), _F32), b8], axis=1)], axis=0)
    rhs = _pad_rows(rhs, kp).astype(_BF16)
    res = lax.dot_general(lhs, rhs, _TN, preferred_element_type=_F32)
    st_out_ref[i] = s_old * res[:, 0:N] + res[:, N:2 * N]

    y = y + dskip_ref[...] * xs_c
    y = y * _silu(z_ref[i])
    ms = _own_block(gmask, jnp.sum(jnp.where(gmask, y * y, 0.0), axis=-1, keepdims=True) * (1.0 / GW))
    y_ref[i] = (y * lax.rsqrt(ms + EPS) * nw_ref[...]).astype(y_ref.dtype)


def _ssd_step(dm, layer, bb, proj3, proj3_dt, state_conv, state_ssd, conv_w, conv_b, dtb, alog,
              dskip_e, nw, e_mat):
    W, GN = dm.W, dm.GN
    nb = proj3.shape[0]
    par = lambda b: (layer, 0, 0)
    in_specs = [
        pl.BlockSpec((bb, 1, W), lambda b: (b, 0, 0)),
        pl.BlockSpec((bb, 1, W), lambda b: (b, 0, 1)),
        pl.BlockSpec((bb, 1, GN), lambda b: (b, 0, 2 * W // GN)),
        pl.BlockSpec((bb, 1, GN), lambda b: (b, 0, 2 * W // GN + 1)),
        pl.BlockSpec((bb, 1, LANES), lambda b: (b, 0, 0)),
        pl.BlockSpec((None, bb, CONV_K - 1, dm.CD), lambda b: (layer, b, 0, 0)),
        pl.BlockSpec((None, bb, W, dm.N), lambda b: (layer, b, 0, 0)),
        pl.BlockSpec((None, CONV_K, dm.CD), par),
        pl.BlockSpec((None, 1, dm.CD), par),
        pl.BlockSpec((None, 1, LANES), par),
        pl.BlockSpec((None, 1, LANES), par),
        pl.BlockSpec((None, 1, W), par),
        pl.BlockSpec((None, 1, W), par),
        pl.BlockSpec((LANES, W), lambda b: (0, 0)),
    ]
    args = [proj3, proj3, proj3, proj3, proj3_dt, state_conv, state_ssd, conv_w, conv_b, dtb, alog,
            dskip_e, nw, e_mat]
    out_specs = [
        pl.BlockSpec((bb, 1, W), lambda b: (b, 0, 0)),
        pl.BlockSpec((bb, CONV_K - 1, dm.CD), lambda b: (b, 0, 0)),
        pl.BlockSpec((None, bb, W, dm.N), lambda b: (layer, b, 0, 0)),
    ]
    out_shape = [
        jax.ShapeDtypeStruct((nb, 1, W), _BF16),
        jax.ShapeDtypeStruct((nb, CONV_K - 1, dm.CD), _F32),
        jax.ShapeDtypeStruct(state_ssd.shape, _F32),
    ]
    return in_specs, args, out_specs, out_shape


def _ret_step_one(dm, i, q_ref, k_ref, v_ref, g_ref, cos_ref, sin_ref, gam_ref, st_ref, nw_ref,
                  y_ref, st_out_ref):
    DK, DV, RH, W, QK = dm.DK, dm.DV, dm.RH, dm.W, dm.QK
    gammas = [float(np.exp(np.float32(lg))) for lg in _ret_log_gammas(RH)]
    cos_t, sin_t = cos_ref[...], sin_ref[...]
    lane = lax.broadcasted_iota(jnp.int32, (1, QK), 1)
    first_half = jnp.bitwise_and(lane, DK - 1) < DK // 2

    def rotate_heads(x):
        partner = jnp.where(first_half, pltpu.roll(x, QK - DK // 2, axis=1), pltpu.roll(x, DK // 2, axis=1))
        return x * cos_t + partner * sin_t

    qr = rotate_heads(q_ref[i])
    kr = rotate_heads(k_ref[i]) * (DK ** -0.5)
    q16 = jnp.concatenate([qr[:, h * DK:(h + 1) * DK] for h in range(RH)], axis=0)
    k16 = jnp.concatenate([kr[:, h * DK:(h + 1) * DK] for h in range(RH)], axis=0)
    v = v_ref[i]
    hmask = _block_mask(RH, W, DV)
    kp = -(-RH // 16) * 16

    s_old = st_ref[i]
    y_all = lax.dot_general(_pad_rows(q16, kp).astype(_BF16), s_old.astype(_BF16), _NT,
                            preferred_element_type=_F32)[0:RH]
    y_old = _own_block(hmask, y_all)
    qk_e = _own_block(hmask, jnp.sum(q16 * k16, axis=-1, keepdims=True))
    y = gam_ref[...] * y_old + qk_e * v
    lhs = _pad_rows(jnp.where(hmask, v, 0.0), kp).astype(_BF16)
    outer = lax.dot_general(lhs, _pad_rows(k16, kp).astype(_BF16), _TN,
                            preferred_element_type=_F32)
    for h in range(RH):
        rows = slice(h * DV, (h + 1) * DV)
        st_out_ref[i, rows, :] = gammas[h] * st_ref[i, rows, :] + outer[rows, :]
    ms = _own_block(hmask, jnp.sum(jnp.where(hmask, y * y, 0.0), axis=-1, keepdims=True) * (1.0 / DV))
    y = y * lax.rsqrt(ms + EPS) * nw_ref[...] * _silu(g_ref[i])
    y_ref[i] = y.astype(y_ref.dtype)


def _ret_step(dm, layer, bb, proj3, state_ret, cos_t, sin_t, gam_e, nw):
    W, QK = dm.W, dm.QK
    nb = proj3.shape[0]
    cst = lambda b: (0, 0)
    in_specs = [
        pl.BlockSpec((bb, 1, QK), lambda b: (b, 0, 0)),
        pl.BlockSpec((bb, 1, QK), lambda b: (b, 0, 1)),
        pl.BlockSpec((bb, 1, W), lambda b: (b, 0, 1)),
        pl.BlockSpec((bb, 1, W), lambda b: (b, 0, 2)),
        pl.BlockSpec((1, QK), cst),
        pl.BlockSpec((1, QK), cst),
        pl.BlockSpec((1, W), cst),
        pl.BlockSpec((None, bb, dm.RH * dm.DV, dm.DK), lambda b: (layer, b, 0, 0)),
        pl.BlockSpec((None, 1, W), lambda b: (layer, 0, 0)),
    ]
    args = [proj3, proj3, proj3, proj3, cos_t, sin_t, gam_e, state_ret, nw]
    out_specs = [
        pl.BlockSpec((bb, 1, W), lambda b: (b, 0, 0)),
        pl.BlockSpec((None, bb, dm.RH * dm.DV, dm.DK), lambda b: (layer, b, 0, 0)),
    ]
    out_shape = [
        jax.ShapeDtypeStruct((nb, 1, W), _BF16),
        jax.ShapeDtypeStruct(state_ret.shape, _F32),
    ]
    return in_specs, args, out_specs, out_shape


N_SSD_IN, N_RET_IN = 14, 9


def _fused_step_kernel(dm, bb, *refs):
    ssd_in = refs[:N_SSD_IN]
    ret_in = refs[N_SSD_IN:N_SSD_IN + N_RET_IN]
    outs = refs[-5:]
    for i in range(bb):
        _ssd_step_one(dm, i, *ssd_in, *outs[:3])
        _ret_step_one(dm, i, *ret_in, *outs[3:])


def _fused_step(dm, bb, ssd_parts, ret_parts, prev_ssd, prev_ret):
    s_in, s_args, s_out, s_shape = ssd_parts
    r_in, r_args, r_out, r_shape = ret_parts
    assert len(s_in) == N_SSD_IN and len(r_in) == N_RET_IN
    in_specs, args = s_in + r_in, s_args + r_args
    aliases = {}
    if prev_ssd is not None:
        in_specs = in_specs + [pl.BlockSpec(memory_space=pl.ANY), pl.BlockSpec(memory_space=pl.ANY)]
        args = args + [prev_ssd, prev_ret]
        aliases = {len(args) - 2: 2, len(args) - 1: 4}
    nb = s_shape[0].shape[0]
    return pl.pallas_call(
        functools.partial(_fused_step_kernel, dm, bb),
        grid=(nb // bb,),
        in_specs=in_specs,
        out_specs=s_out + r_out,
        out_shape=s_shape + r_shape,
        input_output_aliases=aliases,
        compiler_params=_cparams("arbitrary"),
        name="fused_step",
    )(*args)


def _rope_tables(dk, positions):
    half = dk // 2
    inv = 1.0 / (ROPE_BASE ** (jnp.arange(half, dtype=_F32) / half))
    ang = positions.astype(_F32)[:, None] * inv[None, :]
    cos = jnp.cos(ang)
    sin = jnp.sin(ang)
    return jnp.concatenate([cos, cos], axis=1), jnp.concatenate([-sin, sin], axis=1)


def _pad_lanes(v):
    return jnp.pad(v, ((0, 0), (0, LANES - v.shape[1])))[:, None, :]


def kernel(x_prompt, x_sample, state_conv, state_ssd, state_ret, norm_w, w_in, conv_w, conv_b, dt_bias,
           a_log, d_skip, ssd_norm_w, ret_norm_w, w_out, final_norm_w):
    bp, tp, d_model = x_prompt.shape
    bs, ts, _ = x_sample.shape
    depth = norm_w.shape[0]
    assert ts == 1 and tp % CHUNK == 0
    dm = Dims(d_model, dt_bias.shape[1], conv_w.shape[2], state_ret.shape[2], state_ret.shape[3],
              state_ret.shape[4])
    assert w_in.shape[2] == dm.IN_COLS
    W = dm.W

    w_in_t = jnp.swapaxes(w_in, 1, 2)
    off_b = dm.NA + dm.H
    w_a = _transpose_cast_weights(w_in_t, 0, 0, dm.NA, None, 256)
    w_b = _transpose_cast_weights(w_in_t, 0, off_b, dm.NB, None, 256)
    w_dt = [_transpose_cast_weights(w_in_t, l, dm.NA, LANES, off_b, LANES) for l in range(depth)]
    w_o = _cast_weights(w_out, 0, min(1024, 2 * d_model), 512)

    heads = jnp.arange(LANES)[:, None]
    lanes = jnp.arange(W)[None, :]
    e_mat = (lanes // dm.P == heads).astype(_BF16)
    cos_p, sin_p = _rope_tables(dm.DK, jnp.arange(tp))
    cos_s, sin_s = _rope_tables(dm.DK, PAST_LEN + jnp.arange(ts))
    cos_s, sin_s = jnp.tile(cos_s, (1, dm.RH)), jnp.tile(sin_s, (1, dm.RH))
    gam_e = jnp.repeat(jnp.exp(jnp.asarray(_ret_log_gammas(dm.RH), _F32)), dm.DV)[None, :]
    cb3 = conv_b[:, None, :]
    dtb3, alog3 = _pad_lanes(dt_bias), _pad_lanes(a_log)
    dskip3 = jnp.repeat(d_skip, dm.P, axis=1)[:, None, :]
    snw3, rnw3 = ssd_norm_w[:, None, :], ret_norm_w[:, None, :]

    st_ssd_in = state_ssd.reshape(depth, bs, dm.H * dm.P, dm.N)
    st_ret_in = state_ret.reshape(depth, bs, dm.RH * dm.DV, dm.DK)

    xp = x_prompt.reshape(bp * tp, d_model)
    xs = x_sample.reshape(bs * ts, d_model)
    mp = xp.shape[0]
    tm_p = min(1024, mp)
    tm_o, tn_o = min(1024, mp), 256
    tm_n = min(256, mp)
    tn_a = _largest_divisor((1024, 512), dm.NA)
    tn_b = _largest_divisor((1024, 512), dm.NB)
    tn_s = _largest_divisor((2048, 1024, 512), dm.NA, dm.NB)
    tn_so = _largest_divisor((1024, 512), d_model)
    bb = _largest_divisor((2, 1), bs)

    conv_p, ssd_p, ret_p, conv_s = [], [], [], []
    ssd_s = None
    ret_s = None
    for l in range(depth):
        more = l + 1 < depth
        h = _rmsnorm(xp, norm_w[l], _BF16, tm_n)
        proj_a, w_a_next = _inproj(h, w_a, tm_p, tn_a, (w_in_t, l + 1, 0, dm.NA) if more else None)
        proj_b, w_b_next = _inproj(h, w_b, tm_p, tn_b, (w_in_t, l + 1, off_b, dm.NB) if more else None)
        proj_dt, _ = _inproj(h, w_dt[l], tm_p, LANES)
        y_ssd, c_new, s_new = _ssd_prompt(dm, l, proj_a, proj_dt, bp, conv_w, cb3, dtb3, alog3, dskip3, snw3,
                                          e_mat)
        y_ret, r_new = _ret_prompt(dm, l, proj_b, bp, cos_p, sin_p, rnw3)
        xp, w_o_next = _outproj(y_ssd, y_ret, w_o, xp, tm_o, tn_o, (w_out, l + 1) if more else None)
        conv_p.append(c_new)
        ssd_p.append(s_new.reshape(bp, dm.H, dm.P, dm.N))
        ret_p.append(r_new.reshape(bp, dm.RH, dm.DV, dm.DK))

        h = _rmsnorm(xs, norm_w[l], _BF16, bs)
        proj3_a = _inproj(h, w_a, bs, tn_s)[0].reshape(bs, 1, dm.NA)
        proj3_b = _inproj(h, w_b, bs, tn_s)[0].reshape(bs, 1, dm.NB)
        proj3_dt = _inproj(h, w_dt[l], bs, LANES)[0].reshape(bs, 1, LANES)
        y_ssd, c_new, ssd_s, y_ret, ret_s = _fused_step(
            dm, bb,
            _ssd_step(dm, l, bb, proj3_a, proj3_dt, state_conv, st_ssd_in, conv_w, cb3, dtb3, alog3, dskip3,
                      snw3, e_mat),
            _ret_step(dm, l, bb, proj3_b, st_ret_in, cos_s, sin_s, gam_e, rnw3), ssd_s, ret_s)
        xs, _ = _outproj(y_ssd.reshape(bs, W), y_ret.reshape(bs, W), w_o, xs, bs, tn_so)
        conv_s.append(c_new)
        w_a, w_b, w_o = w_a_next, w_b_next, w_o_next

    y_prompt = _rmsnorm(xp, final_norm_w, _F32, tm_n).reshape(bp, tp, d_model)
    y_sample = _rmsnorm(xs, final_norm_w, _F32, bs).reshape(bs, ts, d_model)
    return (y_prompt, y_sample, jnp.stack(conv_p), jnp.stack(ssd_p), jnp.stack(ret_p),
            jnp.stack(conv_s),
            ssd_s.reshape(depth, bs, dm.H, dm.P, dm.N),
            ret_s.reshape(depth, bs, dm.RH, dm.DV, dm.DK))
```
